```python
import jax, jax.numpy as jnp
from jax import lax
import numpy as np

D_MODEL = 4096
BATCH = 4
SEQ = 4096
DEPTH = 2

N_MIXERS = 2
EPS = 1e-6
GM_CHUNK = 128
GM_HEADS = 8
GM_WIDTH = D_MODEL
GM_HEAD_DIM = GM_WIDTH // GM_HEADS
ML_HEADS = 8
ML_QK_DIM = D_MODEL // ML_HEADS // 2
ML_V_DIM = D_MODEL // ML_HEADS
ML_CHUNK = 128
GATE_CAP = 15.0
ML_IN = 2 * ML_HEADS * ML_QK_DIM + 2 * ML_HEADS * ML_V_DIM + 2 * ML_HEADS
D_FF = 4 * D_MODEL
N_A = (DEPTH + 1) // 2
N_B = DEPTH // 2

kernel_name = "hybrid_gmlp_mlstm_sqrelu"


def rmsnorm(x, g):
    xf = x.astype(jnp.float32)
    y = xf * lax.rsqrt(jnp.mean(xf * xf, axis=-1, keepdims=True) + EPS)
    return (y * g.astype(jnp.float32)).astype(x.dtype)


def layernorm(x, g, b):
    xf = x.astype(jnp.float32)
    mu = jnp.mean(xf, axis=-1, keepdims=True)
    xc = xf - mu
    y = xc * lax.rsqrt(jnp.mean(xc * xc, axis=-1, keepdims=True) + EPS)
    return (y * g.astype(jnp.float32) + b.astype(jnp.float32)).astype(x.dtype)


def gmlp_mixer(h, w_in, ln_g, ln_b, w_s, b_s, w_out):
    B, S, _ = h.shape
    nc = S // GM_CHUNK
    z = jax.nn.gelu(h @ w_in, approximate=False)
    u, v = jnp.split(z, 2, axis=-1)
    v = layernorm(v, ln_g, ln_b)
    v = v.reshape(B, nc, GM_CHUNK, GM_HEADS, GM_HEAD_DIM)
    causal = jnp.tril(jnp.ones((GM_CHUNK, GM_CHUNK), dtype=bool))
    w = jnp.where(causal[None], w_s, 0).astype(v.dtype)
    mixed = jnp.einsum('hts,bcshd->bcthd', w, v) + b_s.T[:, :, None].astype(v.dtype)
    gated = u * mixed.reshape(B, S, GM_WIDTH)
    return gated @ w_out


def mlstm_mixer(h, w_in, b_gate, head_g, w_out):
    B, S, _ = h.shape
    H, DK, DV, L = ML_HEADS, ML_QK_DIM, ML_V_DIM, ML_CHUNK
    nc = S // L
    f32 = jnp.float32
    proj = h @ w_in
    q, k, v, o, gates = jnp.split(
        proj, [H * DK, 2 * H * DK, 2 * H * DK + H * DV, 2 * H * DK + 2 * H * DV], axis=-1)
    gates = (gates + b_gate).astype(f32)
    gates = GATE_CAP * jnp.tanh(gates / GATE_CAP)
    i_pre = gates[..., :H]
    log_f = jax.nn.log_sigmoid(gates[..., H:])

    def to_chunks(t, d):
        return t.reshape(B, nc, L, H, d).transpose(1, 0, 3, 2, 4).astype(f32)

    qc = to_chunks(q, DK) * (DK ** -0.5)
    kc = to_chunks(k, DK)
    vc = to_chunks(v, DV)
    ic = i_pre.reshape(B, nc, L, H).transpose(1, 0, 3, 2)
    fc = log_f.reshape(B, nc, L, H).transpose(1, 0, 3, 2)
    causal = jnp.tril(jnp.ones((L, L), dtype=bool))

    def step(carry, xs):
        C, n, m = carry
        q_, k_, v_, i_, lf = xs
        bcum = jnp.cumsum(lf, axis=-1)
        g = bcum[..., -1]
        d_log = bcum[..., :, None] - bcum[..., None, :] + i_[..., None, :]
        d_log = jnp.where(causal, d_log, -jnp.inf)
        inter_log = bcum + m[..., None]
        m_t = jnp.maximum(inter_log, jnp.max(d_log, axis=-1))
        dmat = jnp.exp(d_log - m_t[..., None])
        inter_w = jnp.exp(inter_log - m_t)
        s = jnp.einsum('bhtk,bhsk->bhts', q_, k_) * dmat
        num = (inter_w[..., None] * jnp.einsum('bhtk,bhkv->bhtv', q_, C)
               + jnp.einsum('bhts,bhsv->bhtv', s, v_))
        den = inter_w * jnp.einsum('bhtk,bhk->bht', q_, n) + jnp.sum(s, axis=-1)
        h_out = num / jnp.maximum(jnp.abs(den), jnp.exp(-m_t))[..., None]
        a = g[..., None] - bcum + i_
        m_new = jnp.maximum(g + m, jnp.max(a, axis=-1))
        decay = jnp.exp(g + m - m_new)
        kw = k_ * jnp.exp(a - m_new[..., None])[..., None]
        C_new = decay[..., None, None] * C + jnp.einsum('bhsk,bhsv->bhkv', kw, v_)
        n_new = decay[..., None] * n + jnp.sum(kw, axis=2)
        return (C_new, n_new, m_new), h_out

    init = (jnp.zeros((B, H, DK, DV), f32), jnp.zeros((B, H, DK), f32), jnp.zeros((B, H), f32))
    _, hs = lax.scan(step, init, (qc, kc, vc, ic, fc))
    hs = hs.transpose(1, 0, 3, 2, 4).reshape(B, S, H, DV)
    hs = rmsnorm(hs, head_g).astype(h.dtype)
    hs = hs.reshape(B, S, H * DV) * jax.nn.sigmoid(o)
    return hs @ w_out


def sqrelu_mlp(h, w_up, w_down):
    a = jax.nn.relu(h @ w_up)
    return (a * a) @ w_down


def setup_inputs(seed: int = 0) -> dict:
    key = jax.random.key(seed)
    ks = jax.random.split(key, 20)
    nrm = jax.random.normal
    f32 = jnp.float32
    x = nrm(ks[0], (BATCH, SEQ, D_MODEL), f32)
    norm_mix = 1.0 + 0.05 * nrm(ks[1], (DEPTH, D_MODEL), f32)
    norm_ffn = 1.0 + 0.05 * nrm(ks[2], (DEPTH, D_MODEL), f32)
    gm_w_in = nrm(ks[3], (N_A, D_MODEL, 2 * GM_WIDTH), f32) * D_MODEL ** -0.5
    gm_ln_g = 1.0 + 0.05 * nrm(ks[4], (N_A, GM_WIDTH), f32)
    gm_ln_b = 0.02 * nrm(ks[5], (N_A, GM_WIDTH), f32)
    gm_w_s = nrm(ks[6], (N_A, GM_HEADS, GM_CHUNK, GM_CHUNK), f32) * GM_CHUNK ** -0.5
    gm_b_s = 1.0 + 0.05 * nrm(ks[7], (N_A, GM_HEADS, GM_CHUNK), f32)
    gm_w_out = nrm(ks[8], (N_A, GM_WIDTH, D_MODEL), f32) * GM_WIDTH ** -0.5
    ml_w_in = nrm(ks[9], (N_B, D_MODEL, ML_IN), f32) * D_MODEL ** -0.5
    b_i = 0.1 * nrm(ks[10], (N_B, ML_HEADS), f32)
    b_f = 3.0 + 0.5 * nrm(ks[11], (N_B, ML_HEADS), f32)
    ml_b_gate = jnp.concatenate([b_i, b_f], axis=-1)
    ml_head_g = 1.0 + 0.05 * nrm(ks[12], (N_B, ML_HEADS, ML_V_DIM), f32)
    ml_w_out = nrm(ks[13], (N_B, ML_HEADS * ML_V_DIM, D_MODEL), f32) * (ML_HEADS * ML_V_DIM) ** -0.5
    ffn_w_up = nrm(ks[14], (DEPTH, D_MODEL, D_FF), f32) * D_MODEL ** -0.5
    ffn_w_down = nrm(ks[15], (DEPTH, D_FF, D_MODEL), f32) * (0.5 * D_FF ** -0.5)
    norm_final = 1.0 + 0.05 * nrm(ks[16], (D_MODEL,), f32)
    return {"x": x, "norm_mix": norm_mix, "norm_ffn": norm_ffn,
            "gm_w_in": gm_w_in, "gm_ln_g": gm_ln_g, "gm_ln_b": gm_ln_b,
            "gm_w_s": gm_w_s, "gm_b_s": gm_b_s, "gm_w_out": gm_w_out,
            "ml_w_in": ml_w_in, "ml_b_gate": ml_b_gate, "ml_head_g": ml_head_g,
            "ml_w_out": ml_w_out, "ffn_w_up": ffn_w_up, "ffn_w_down": ffn_w_down,
            "norm_final": norm_final}


def reference(x, norm_mix, norm_ffn, gm_w_in, gm_ln_g, gm_ln_b, gm_w_s, gm_b_s, gm_w_out,
              ml_w_in, ml_b_gate, ml_head_g, ml_w_out, ffn_w_up, ffn_w_down, norm_final):
    h = x
    for i in range(DEPTH):
        hn = rmsnorm(h, norm_mix[i])
        j = i // N_MIXERS
        if i % N_MIXERS == 0:
            mix = gmlp_mixer(hn, gm_w_in[j], gm_ln_g[j], gm_ln_b[j], gm_w_s[j], gm_b_s[j], gm_w_out[j])
        else:
            mix = mlstm_mixer(hn, ml_w_in[j], ml_b_gate[j], ml_head_g[j], ml_w_out[j])
        h = h + mix
        h = h + sqrelu_mlp(rmsnorm(h, norm_ffn[i]), ffn_w_up[i], ffn_w_down[i])
    return rmsnorm(h, norm_final)
```

```python
import functools
import math

import jax
import jax.numpy as jnp
from jax import lax
from jax.experimental import pallas as pl
from jax.experimental.pallas import tpu as pltpu

F32 = jnp.float32
BF16 = jnp.bfloat16

EPS = 1e-6
GM_CHUNK = 128
GM_HEADS = 8
ML_HEADS = 8
ML_CHUNK = 128
GATE_CAP = 15.0
LANES = 128

VMEM_LIMIT_BYTES = 60 * 1024 * 1024


def _params(semantics):
    return pltpu.CompilerParams(
        dimension_semantics=semantics, vmem_limit_bytes=VMEM_LIMIT_BYTES)


def _rms_scale(x):
    return lax.rsqrt(jnp.mean(x * x, axis=-1, keepdims=True) + EPS)


def _rmsnorm_kernel(x_ref, g_ref, o_ref):
    x = x_ref[...]
    o_ref[...] = (x * _rms_scale(x) * g_ref[...]).astype(o_ref.dtype)


def rmsnorm(x, g, out_dtype, rows):
    t, d = x.shape
    return pl.pallas_call(
        _rmsnorm_kernel,
        grid=(t // rows,),
        in_specs=[pl.BlockSpec((rows, d), lambda i: (i, 0)),
                  pl.BlockSpec((1, d), lambda i: (0, 0))],
        out_specs=pl.BlockSpec((rows, d), lambda i: (i, 0)),
        out_shape=jax.ShapeDtypeStruct((t, d), out_dtype),
        compiler_params=_params(("arbitrary",)),
        name="rmsnorm",
    )(x, g.reshape(1, d))


def _split3_bf16(x):
    hi = x.astype(BF16)
    r = x - hi.astype(F32)
    mid = r.astype(BF16)
    lo = (r - mid.astype(F32)).astype(BF16)
    return hi, mid, lo


def _norm_gates_kernel(x_ref, g_ref, wg_ref, bg_ref, hn_ref, gate_ref, *, heads):
    x = x_ref[...]
    hn = (x * _rms_scale(x) * g_ref[...]).astype(BF16)
    hn_ref[...] = hn
    pre = jnp.dot(hn, wg_ref[...], preferred_element_type=F32) + bg_ref[...]
    pre = GATE_CAP * jnp.tanh(pre / GATE_CAP)
    log_f = jnp.minimum(pre, 0.0) - jnp.log1p(jnp.exp(-jnp.abs(pre)))
    row = lax.broadcasted_iota(jnp.int32, (ML_CHUNK, ML_CHUNK), 0)
    col = lax.broadcasted_iota(jnp.int32, (ML_CHUNK, ML_CHUNK), 1)
    tril = (col <= row).astype(BF16)
    lane = lax.broadcasted_iota(jnp.int32, (ML_CHUNK, LANES), 1)
    for c in range(x.shape[0] // ML_CHUNK):
        sl = slice(c * ML_CHUNK, (c + 1) * ML_CHUNK)
        cum = sum(jnp.dot(tril, p, preferred_element_type=F32)
                  for p in _split3_bf16(log_f[sl]))
        gate_ref[sl, :] = jnp.where(lane < heads, pre[sl], cum)


def norm_gates(x, g, w_gate, b_gate, rows):
    t, d = x.shape
    n_gate = w_gate.shape[1]
    heads = n_gate // 2
    wg = jnp.zeros((d, LANES), BF16).at[:, :n_gate].set(w_gate.astype(BF16))
    bg = jnp.zeros((1, LANES), F32).at[0, :n_gate].set(b_gate)
    return pl.pallas_call(
        functools.partial(_norm_gates_kernel, heads=heads),
        grid=(t // rows,),
        in_specs=[pl.BlockSpec((rows, d), lambda i: (i, 0)),
                  pl.BlockSpec((1, d), lambda i: (0, 0)),
                  pl.BlockSpec((d, LANES), lambda i: (0, 0)),
                  pl.BlockSpec((1, LANES), lambda i: (0, 0))],
        out_specs=[pl.BlockSpec((rows, d), lambda i: (i, 0)),
                   pl.BlockSpec((rows, LANES), lambda i: (i, 0))],
        out_shape=[jax.ShapeDtypeStruct((t, d), BF16),
                   jax.ShapeDtypeStruct((t, LANES), F32)],
        compiler_params=_params(("arbitrary",)),
        name="norm_gates",
    )(x, g.reshape(1, d), wg, bg)


def _gelu(x):
    return 0.5 * x * (1.0 + lax.erf(x * math.sqrt(0.5)))


def _sqrelu(x):
    r = jnp.maximum(x, 0.0)
    return r * r


def _identity(x):
    return x


def _mm_wres_kernel(a_ref, w_ref, *refs, epilogue, has_res):
    if has_res:
        res_ref, o_ref, wq_ref = refs
    else:
        o_ref, wq_ref = refs

    @pl.when(pl.program_id(1) == 0)
    def _():
        wq_ref[...] = w_ref[...].astype(BF16)

    acc = jnp.dot(a_ref[...], wq_ref[...], preferred_element_type=F32)
    out = epilogue(acc)
    if has_res:
        out = res_ref[...] + out
    o_ref[...] = out.astype(o_ref.dtype)


def matmul_wres(a, w, layer, *, n_out, tm, tn, epilogue, out_dtype, res=None, name):
    m, k = a.shape
    assert w.shape[1] == k and n_out % tn == 0 and m % tm == 0
    in_specs = [pl.BlockSpec((tm, k), lambda j, i: (i, 0)),
                pl.BlockSpec((None, k, tn), lambda j, i: (layer, 0, j))]
    args = [a, w]
    if res is not None:
        in_specs.append(pl.BlockSpec((tm, tn), lambda j, i: (i, j)))
        args.append(res)
    return pl.pallas_call(
        functools.partial(_mm_wres_kernel, epilogue=epilogue, has_res=res is not None),
        grid=(n_out // tn, m // tm),
        in_specs=in_specs,
        out_specs=pl.BlockSpec((tm, tn), lambda j, i: (i, j)),
        out_shape=jax.ShapeDtypeStruct((m, n_out), out_dtype),
        scratch_shapes=[pltpu.VMEM((k, tn), BF16)],
        compiler_params=_params(("arbitrary", "arbitrary")),
        name=name,
    )(*args)


def _mm_ksplit_kernel(a_ref, w_ref, res_ref, o_ref, acc_ref):
    kk = pl.program_id(2)
    part = jnp.dot(a_ref[...], w_ref[...].astype(BF16), preferred_element_type=F32)

    @pl.when(kk == 0)
    def _():
        acc_ref[...] = part

    @pl.when(kk > 0)
    def _():
        acc_ref[...] += part

    @pl.when(kk == pl.num_programs(2) - 1)
    def _():
        o_ref[...] = res_ref[...] + acc_ref[...]


def matmul_ksplit_res(a, w, layer, res, *, tm, tn, tk, name):
    m, k = a.shape
    n = w.shape[2]
    assert w.shape[1] == k and m % tm == 0 and n % tn == 0 and k % tk == 0
    return pl.pallas_call(
        _mm_ksplit_kernel,
        grid=(n // tn, m // tm, k // tk),
        in_specs=[pl.BlockSpec((tm, tk), lambda j, i, kk: (i, kk)),
                  pl.BlockSpec((None, tk, tn), lambda j, i, kk: (layer, kk, j)),
                  pl.BlockSpec((tm, tn), lambda j, i, kk: (i, j))],
        out_specs=pl.BlockSpec((tm, tn), lambda j, i, kk: (i, j)),
        out_shape=jax.ShapeDtypeStruct((m, n), F32),
        scratch_shapes=[pltpu.VMEM((tm, tn), F32)],
        compiler_params=_params(("arbitrary", "arbitrary", "arbitrary")),
        name=name,
    )(a, w, res)


def _gmlp_gate_kernel(u_ref, v_ref, lng_ref, lnb_ref, ws_ref, bs_ref, o_ref):
    v = v_ref[...].astype(F32)
    mu = jnp.mean(v, axis=-1, keepdims=True)
    vc = v - mu
    vn = vc * lax.rsqrt(jnp.mean(vc * vc, axis=-1, keepdims=True) + EPS)
    vn = (vn * lng_ref[...] + lnb_ref[...]).astype(BF16)
    row = lax.broadcasted_iota(jnp.int32, (GM_CHUNK, GM_CHUNK), 0)
    col = lax.broadcasted_iota(jnp.int32, (GM_CHUNK, GM_CHUNK), 1)
    causal = col <= row
    hd = v.shape[1] // GM_HEADS
    for h in range(GM_HEADS):
        sl = slice(h * hd, (h + 1) * hd)
        w = jnp.where(causal, ws_ref[h], 0.0).astype(BF16)
        mixed = jnp.dot(w, vn[:, sl], preferred_element_type=F32) + bs_ref[:, h:h + 1]
        o_ref[:, sl] = (u_ref[:, sl].astype(F32) * mixed).astype(o_ref.dtype)


def gmlp_gate(z, ln_g, ln_b, w_s, b_s):
    t, two_w = z.shape
    width = two_w // 2
    return pl.pallas_call(
        _gmlp_gate_kernel,
        grid=(t // GM_CHUNK,),
        in_specs=[pl.BlockSpec((GM_CHUNK, width), lambda i: (i, 0)),
                  pl.BlockSpec((GM_CHUNK, width), lambda i: (i, 1)),
                  pl.BlockSpec((1, width), lambda i: (0, 0)),
                  pl.BlockSpec((1, width), lambda i: (0, 0)),
                  pl.BlockSpec((GM_HEADS, GM_CHUNK, GM_CHUNK), lambda i: (0, 0, 0)),
                  pl.BlockSpec((GM_CHUNK, GM_HEADS), lambda i: (0, 0))],
        out_specs=pl.BlockSpec((GM_CHUNK, width), lambda i: (i, 0)),
        out_shape=jax.ShapeDtypeStruct((t, width), BF16),
        compiler_params=_params(("arbitrary",)),
        name="gmlp_gate",
    )(z, z, ln_g.reshape(1, width), ln_b.reshape(1, width), w_s, b_s.T)


def _mlstm_kernel(q_ref, k_ref, v_ref, o_ref, gate_ref, hg_ref, out_ref,
                  c_ref, n_ref, m_ref, *, heads):
    dk = q_ref.shape[1] // heads
    dv = v_ref.shape[1] // heads
    L = ML_CHUNK

    @pl.when(pl.program_id(1) == 0)
    def _():
        c_ref[...] = jnp.zeros_like(c_ref)
        n_ref[...] = jnp.zeros_like(n_ref)
        m_ref[...] = jnp.zeros_like(m_ref)

    gates = gate_ref[...]
    gates_t = gates.T
    row = lax.broadcasted_iota(jnp.int32, (L, L), 0)
    col = lax.broadcasted_iota(jnp.int32, (L, L), 1)
    causal = col <= row

    for h in range(heads):
        q = q_ref[:, h * dk:(h + 1) * dk] * (dk ** -0.5)
        k = k_ref[:, h * dk:(h + 1) * dk]
        v = v_ref[:, h * dv:(h + 1) * dv]
        i_col = gates[:, h:h + 1]
        b_col = gates[:, heads + h:heads + h + 1]
        i_row = gates_t[h:h + 1, :]
        b_row = gates_t[heads + h:heads + h + 1, :]
        g = b_col[L - 1:L, :]
        m = m_ref[h:h + 1, 0:1]
        c_state = c_ref[h]
        n_state = n_ref[h:h + 1, :]

        d_log = jnp.where(causal, b_col - b_row + i_row, -jnp.inf)
        inter_log = b_col + m
        m_t = jnp.maximum(inter_log, jnp.max(d_log, axis=-1, keepdims=True))
        dmat = jnp.exp(d_log - m_t)
        inter_w = jnp.exp(inter_log - m_t)
        s = lax.dot_general(q, k, (((1,), (1,)), ((), ())),
                            preferred_element_type=F32) * dmat
        num = (inter_w * jnp.dot(q, c_state.astype(BF16), preferred_element_type=F32)
               + jnp.dot(s.astype(BF16), v, preferred_element_type=F32))
        n_mxu = n_state.astype(BF16).astype(F32)
        qn = jnp.sum(q.astype(F32) * n_mxu, axis=-1, keepdims=True)
        den = inter_w * qn + jnp.sum(s, axis=-1, keepdims=True)
        h_out = num / jnp.maximum(jnp.abs(den), jnp.exp(-m_t))

        a_col = g - b_col + i_col
        m_new = jnp.maximum(g + m, jnp.max(a_col, axis=0, keepdims=True))
        decay = jnp.exp(g + m - m_new)
        kw = k.astype(F32) * jnp.exp(a_col - m_new)
        c_ref[h] = decay * c_state + lax.dot_general(
            kw.astype(BF16), v, (((0,), (0,)), ((), ())), preferred_element_type=F32)
        n_ref[h:h + 1, :] = decay * n_state + jnp.sum(kw, axis=0, keepdims=True)
        m_ref[h:h + 1, :] = jnp.broadcast_to(m_new, (1, LANES))

        hs = h_out * _rms_scale(h_out) * hg_ref[:, h * dv:(h + 1) * dv]
        og = jax.nn.sigmoid(o_ref[:, h * dv:(h + 1) * dv].astype(F32))
        out_ref[:, h * dv:(h + 1) * dv] = (hs * og).astype(out_ref.dtype)


def mlstm(proj, gates, head_g, *, batch, heads, dk, dv):
    t = proj.shape[0]
    nc = t // batch // ML_CHUNK
    qw, vw = heads * dk, heads * dv
    assert qw * 2 == vw
    row_blk = lambda b, c: b * nc + c
    return pl.pallas_call(
        functools.partial(_mlstm_kernel, heads=heads),
        grid=(batch, nc),
        in_specs=[pl.BlockSpec((ML_CHUNK, qw), lambda b, c: (row_blk(b, c), 0)),
                  pl.BlockSpec((ML_CHUNK, qw), lambda b, c: (row_blk(b, c), 1)),
                  pl.BlockSpec((ML_CHUNK, vw), lambda b, c: (row_blk(b, c), 1)),
                  pl.BlockSpec((ML_CHUNK, vw), lambda b, c: (row_blk(b, c), 2)),
                  pl.BlockSpec((ML_CHUNK, LANES), lambda b, c: (row_blk(b, c), 0)),
                  pl.BlockSpec((1, vw), lambda b, c: (0, 0))],
        out_specs=pl.BlockSpec((ML_CHUNK, vw), lambda b, c: (row_blk(b, c), 0)),
        out_shape=jax.ShapeDtypeStruct((t, vw), BF16),
        scratch_shapes=[pltpu.VMEM((heads, dk, dv), F32),
                        pltpu.VMEM((heads, dk), F32),
                        pltpu.VMEM((heads, LANES), F32)],
        compiler_params=_params(("arbitrary", "arbitrary")),
        name="mlstm",
    )(proj, proj, proj, proj, gates, head_g.reshape(1, vw))


NORM_ROWS = 256
MM_TM = 1024
MM_TN = 512
KS_TM = 1024
KS_TN = 1024
KS_TK = 1024


def _ffn(h, g, w_up, w_down, layer):
    hn = rmsnorm(h, g, BF16, NORM_ROWS)
    a = matmul_wres(hn, w_up, layer, n_out=w_up.shape[2], tm=MM_TM, tn=MM_TN,
                    epilogue=_sqrelu, out_dtype=BF16, name="ffn_up")
    return matmul_ksplit_res(a, w_down, layer, h, tm=KS_TM, tn=KS_TN, tk=KS_TK,
                             name="ffn_down")


def kernel(x, norm_mix, norm_ffn, gm_w_in, gm_ln_g, gm_ln_b, gm_w_s, gm_b_s, gm_w_out,
           ml_w_in, ml_b_gate, ml_head_g, ml_w_out, ffn_w_up, ffn_w_down, norm_final):
    batch, seq, d = x.shape
    h = x.reshape(batch * seq, d)

    hn = rmsnorm(h, norm_mix[0], BF16, NORM_ROWS)
    z = matmul_wres(hn, gm_w_in, 0, n_out=gm_w_in.shape[2], tm=MM_TM, tn=MM_TN,
                    epilogue=_gelu, out_dtype=BF16, name="gmlp_in")
    gated = gmlp_gate(z, gm_ln_g[0], gm_ln_b[0], gm_w_s[0], gm_b_s[0])
    h = matmul_wres(gated, gm_w_out, 0, n_out=d, tm=MM_TM, tn=MM_TN,
                    epilogue=_identity, out_dtype=F32, res=h, name="gmlp_out")
    h = _ffn(h, norm_ffn[0], ffn_w_up, ffn_w_down, 0)

    heads = ML_HEADS
    dv = ml_head_g.shape[2]
    dk = dv // 2
    n_main = 2 * heads * dk + 2 * heads * dv
    hn, gates = norm_gates(h, norm_mix[1], ml_w_in[0, :, n_main:], ml_b_gate[0], NORM_ROWS)
    proj = matmul_wres(hn, ml_w_in, 0, n_out=n_main, tm=MM_TM, tn=MM_TN,
                       epilogue=_identity, out_dtype=BF16, name="mlstm_in")
    hs = mlstm(proj, gates, ml_head_g[0], batch=batch, heads=heads, dk=dk, dv=dv)
    h = matmul_wres(hs, ml_w_out, 0, n_out=d, tm=MM_TM, tn=MM_TN,
                    epilogue=_identity, out_dtype=F32, res=h, name="mlstm_out")
    h = _ffn(h, norm_ffn[1], ffn_w_up, ffn_w_down, 1)

    out = rmsnorm(h, norm_final, F32, NORM_ROWS)
    return out.reshape(batch, seq, d)
```

```python
import functools
import math

import jax
import jax.numpy as jnp
from jax import lax
from jax.experimental import pallas as pl
from jax.experimental.pallas import tpu as pltpu

F32 = jnp.float32
BF16 = jnp.bfloat16

EPS = 1e-6
GM_CHUNK = 128
GM_HEADS = 8
ML_HEADS = 8
ML_CHUNK = 128
GATE_CAP = 15.0
LANES = 128
BF16_SUBLANES = 16

VMEM_LIMIT_BYTES = 60 * 1024 * 1024

NT_DIMS = (((1,), (1,)), ((), ()))


def _params(semantics):
    return pltpu.CompilerParams(
        dimension_semantics=semantics, vmem_limit_bytes=VMEM_LIMIT_BYTES)


def _rms_scale(x):
    return lax.rsqrt(jnp.mean(x * x, axis=-1, keepdims=True) + EPS)


def _rmsnorm_kernel(x_ref, g_ref, o_ref):
    x = x_ref[...]
    o_ref[...] = (x * _rms_scale(x) * g_ref[...]).astype(o_ref.dtype)


def rmsnorm(x, g, out_dtype, rows):
    t, d = x.shape
    return pl.pallas_call(
        _rmsnorm_kernel,
        grid=(t // rows,),
        in_specs=[pl.BlockSpec((rows, d), lambda i: (i, 0)),
                  pl.BlockSpec((1, d), lambda i: (0, 0))],
        out_specs=pl.BlockSpec((rows, d), lambda i: (i, 0)),
        out_shape=jax.ShapeDtypeStruct((t, d), out_dtype),
        compiler_params=_params(("arbitrary",)),
        name="rmsnorm",
    )(x, g.reshape(1, d))


def _split3_bf16(x):
    hi = x.astype(BF16)
    r = x - hi.astype(F32)
    mid = r.astype(BF16)
    lo = (r - mid.astype(F32)).astype(BF16)
    return hi, mid, lo


def _norm_gates_kernel(x_ref, g_ref, wg_ref, bg_ref, hn_ref, gate_ref, *, heads):
    x = x_ref[...]
    hn = (x * _rms_scale(x) * g_ref[...]).astype(BF16)
    hn_ref[...] = hn
    pre = lax.dot_general(hn, wg_ref[...], NT_DIMS, preferred_element_type=F32) + bg_ref[...]
    pre = GATE_CAP * jnp.tanh(pre / GATE_CAP)
    log_f = jnp.minimum(pre, 0.0) - jnp.log1p(jnp.exp(-jnp.abs(pre)))
    row = lax.broadcasted_iota(jnp.int32, (ML_CHUNK, ML_CHUNK), 0)
    col = lax.broadcasted_iota(jnp.int32, (ML_CHUNK, ML_CHUNK), 1)
    tril = (col <= row).astype(BF16)
    lane = lax.broadcasted_iota(jnp.int32, (ML_CHUNK, LANES), 1)
    for c in range(x.shape[0] // ML_CHUNK):
        sl = slice(c * ML_CHUNK, (c + 1) * ML_CHUNK)
        cum = sum(jnp.dot(tril, p, preferred_element_type=F32)
                  for p in _split3_bf16(log_f[sl]))
        gate_ref[sl, :] = jnp.where(lane < heads, pre[sl], cum)


def norm_gates(x, g, w_gate_t, b_gate, rows):
    t, d = x.shape
    n_gate = w_gate_t.shape[0]
    heads = n_gate // 2
    wg = jnp.zeros((LANES, d), BF16).at[:n_gate, :].set(w_gate_t.astype(BF16))
    bg = jnp.zeros((1, LANES), F32).at[0, :n_gate].set(b_gate)
    return pl.pallas_call(
        functools.partial(_norm_gates_kernel, heads=heads),
        grid=(t // rows,),
        in_specs=[pl.BlockSpec((rows, d), lambda i: (i, 0)),
                  pl.BlockSpec((1, d), lambda i: (0, 0)),
                  pl.BlockSpec((LANES, d), lambda i: (0, 0)),
                  pl.BlockSpec((1, LANES), lambda i: (0, 0))],
        out_specs=[pl.BlockSpec((rows, d), lambda i: (i, 0)),
                   pl.BlockSpec((rows, LANES), lambda i: (i, 0))],
        out_shape=[jax.ShapeDtypeStruct((t, d), BF16),
                   jax.ShapeDtypeStruct((t, LANES), F32)],
        compiler_params=_params(("arbitrary",)),
        name="norm_gates",
    )(x, g.reshape(1, d), wg, bg)


def _gelu(x):
    return 0.5 * x * (1.0 + lax.erf(x * math.sqrt(0.5)))


def _sqrelu(x):
    r = jnp.maximum(x, 0.0)
    return r * r


def _identity(x):
    return x


class SideCast:
    def __init__(self, src, layer, n_rows, block_rows):
        assert n_rows % block_rows == 0 and block_rows % BF16_SUBLANES == 0
        self.src, self.layer, self.block_rows = src, layer, block_rows
        self.n_blocks = n_rows // block_rows
        self.cols = src.shape[2]
        self.out_shape = jax.ShapeDtypeStruct((n_rows, self.cols), BF16)

    def specs(self, step_of, n_steps):
        assert self.n_blocks <= n_steps
        last, layer = self.n_blocks - 1, self.layer
        blk = lambda *g: jnp.minimum(step_of(*g), last)
        return (pl.BlockSpec((None, self.block_rows, self.cols),
                             lambda *g: (layer, blk(*g), 0)),
                pl.BlockSpec((self.block_rows, self.cols), lambda *g: (blk(*g), 0)))


def _do_side_casts(src_refs, dst_refs):
    for s_ref, d_ref in zip(src_refs, dst_refs):
        d_ref[...] = s_ref[...].astype(BF16)


def _mm_kernel(*refs, epilogue, has_res, n_cast, w_is_f32, transposed_w):
    a_ref, w_ref = refs[:2]
    pos = 2
    res_ref = None
    if has_res:
        res_ref = refs[pos]
        pos += 1
    cast_src = refs[pos:pos + n_cast]
    pos += n_cast
    o_ref = refs[pos]
    cast_dst = refs[pos + 1:pos + 1 + n_cast]
    pos += 1 + n_cast

    if w_is_f32:
        wq_ref = refs[pos]

        @pl.when(pl.program_id(1) == 0)
        def _():
            wq_ref[...] = w_ref[...].astype(BF16)

        w = wq_ref[...]
    else:
        w = w_ref[...]

    if transposed_w:
        acc = lax.dot_general(a_ref[...], w, NT_DIMS, preferred_element_type=F32)
    else:
        acc = jnp.dot(a_ref[...], w, preferred_element_type=F32)
    out = epilogue(acc)
    if has_res:
        out = res_ref[...] + out
    o_ref[...] = out.astype(o_ref.dtype)
    _do_side_casts(cast_src, cast_dst)


def matmul(a, w, *, n_out, tm, tn, epilogue, out_dtype, name, layer=None, res=None,
           transposed_w=False, casts=()):
    m, k = a.shape
    w_is_f32 = layer is not None
    assert n_out % tn == 0 and m % tm == 0 and not (w_is_f32 and transposed_w)
    ni, nj = m // tm, n_out // tn
    if w_is_f32:
        assert w.shape[1] == k
        w_spec = pl.BlockSpec((None, k, tn), lambda j, i: (layer, 0, j))
    elif transposed_w:
        assert w.shape[1] == k
        w_spec = pl.BlockSpec((tn, k), lambda j, i: (j, 0))
    else:
        assert w.shape[0] == k
        w_spec = pl.BlockSpec((k, tn), lambda j, i: (0, j))
    in_specs = [pl.BlockSpec((tm, k), lambda j, i: (i, 0)), w_spec]
    args = [a, w]
    if res is not None:
        in_specs.append(pl.BlockSpec((tm, tn), lambda j, i: (i, j)))
        args.append(res)
    out_specs = [pl.BlockSpec((tm, tn), lambda j, i: (i, j))]
    out_shape = [jax.ShapeDtypeStruct((m, n_out), out_dtype)]
    for c in casts:
        i_spec, o_spec = c.specs(lambda j, i: j * ni + i, ni * nj)
        in_specs.append(i_spec)
        args.append(c.src)
        out_specs.append(o_spec)
        out_shape.append(c.out_shape)
    return pl.pallas_call(
        functools.partial(_mm_kernel, epilogue=epilogue, has_res=res is not None,
                          n_cast=len(casts), w_is_f32=w_is_f32, transposed_w=transposed_w),
        grid=(nj, ni),
        in_specs=in_specs,
        out_specs=out_specs,
        out_shape=out_shape,
        scratch_shapes=[pltpu.VMEM((k, tn), BF16)] if w_is_f32 else [],
        compiler_params=_params(("arbitrary", "arbitrary")),
        name=name,
    )(*args)


def _mm_ksplit_kernel(*refs, n_cast):
    a_ref, w_ref, res_ref = refs[:3]
    cast_src = refs[3:3 + n_cast]
    o_ref = refs[3 + n_cast]
    cast_dst = refs[4 + n_cast:4 + 2 * n_cast]

    @pl.when(pl.program_id(2) == 0)
    def _():
        o_ref[...] = res_ref[...]

    o_ref[...] += jnp.dot(a_ref[...], w_ref[...], preferred_element_type=F32)
    _do_side_casts(cast_src, cast_dst)


def matmul_ksplit_res(a, w, res, *, tm, tn, tk, name, casts=()):
    m, k = a.shape
    n = w.shape[1]
    assert w.shape[0] == k and m % tm == 0 and n % tn == 0 and k % tk == 0
    nj, ni, nk = n // tn, m // tm, k // tk
    in_specs = [pl.BlockSpec((tm, tk), lambda j, i, kk: (i, kk)),
                pl.BlockSpec((tk, tn), lambda j, i, kk: (kk, j)),
                pl.BlockSpec((tm, tn), lambda j, i, kk: (i, j))]
    args = [a, w, res]
    out_specs = [pl.BlockSpec((tm, tn), lambda j, i, kk: (i, j))]
    out_shape = [jax.ShapeDtypeStruct((m, n), F32)]
    for c in casts:
        i_spec, o_spec = c.specs(lambda j, i, kk: (j * ni + i) * nk + kk, nj * ni * nk)
        in_specs.append(i_spec)
        args.append(c.src)
        out_specs.append(o_spec)
        out_shape.append(c.out_shape)
    return pl.pallas_call(
        functools.partial(_mm_ksplit_kernel, n_cast=len(casts)),
        grid=(nj, ni, nk),
        in_specs=in_specs,
        out_specs=out_specs,
        out_shape=out_shape,
        compiler_params=_params(("arbitrary", "arbitrary", "arbitrary")),
        name=name,
    )(*args)


def _gmlp_gate_kernel(u_ref, v_ref, lng_ref, lnb_ref, ws_ref, bs_ref, o_ref):
    v = v_ref[...].astype(F32)
    mu = jnp.mean(v, axis=-1, keepdims=True)
    vc = v - mu
    vn = vc * lax.rsqrt(jnp.mean(vc * vc, axis=-1, keepdims=True) + EPS)
    vn = (vn * lng_ref[...] + lnb_ref[...]).astype(BF16)
    row = lax.broadcasted_iota(jnp.int32, (GM_CHUNK, GM_CHUNK), 0)
    col = lax.broadcasted_iota(jnp.int32, (GM_CHUNK, GM_CHUNK), 1)
    causal = col <= row
    hd = v.shape[1] // GM_HEADS
    for h in range(GM_HEADS):
        sl = slice(h * hd, (h + 1) * hd)
        w = jnp.where(causal, ws_ref[h], 0.0).astype(BF16)
        mixed = jnp.dot(w, vn[:, sl], preferred_element_type=F32) + bs_ref[:, h:h + 1]
        o_ref[:, sl] = (u_ref[:, sl].astype(F32) * mixed).astype(o_ref.dtype)


def gmlp_gate(z, ln_g, ln_b, w_s, b_s):
    t, two_w = z.shape
    width = two_w // 2
    return pl.pallas_call(
        _gmlp_gate_kernel,
        grid=(t // GM_CHUNK,),
        in_specs=[pl.BlockSpec((GM_CHUNK, width), lambda i: (i, 0)),
                  pl.BlockSpec((GM_CHUNK, width), lambda i: (i, 1)),
                  pl.BlockSpec((1, width), lambda i: (0, 0)),
                  pl.BlockSpec((1, width), lambda i: (0, 0)),
                  pl.BlockSpec((GM_HEADS, GM_CHUNK, GM_CHUNK), lambda i: (0, 0, 0)),
                  pl.BlockSpec((GM_CHUNK, GM_HEADS), lambda i: (0, 0))],
        out_specs=pl.BlockSpec((GM_CHUNK, width), lambda i: (i, 0)),
        out_shape=jax.ShapeDtypeStruct((t, width), BF16),
        compiler_params=_params(("arbitrary",)),
        name="gmlp_gate",
    )(z, z, ln_g.reshape(1, width), ln_b.reshape(1, width), w_s, b_s.T)


def _mlstm_kernel(q_ref, k_ref, v_ref, o_ref, gate_ref, hg_ref, out_ref,
                  c_ref, n_ref, m_ref, *, heads):
    dk = q_ref.shape[1] // heads
    dv = v_ref.shape[1] // heads
    L = ML_CHUNK

    @pl.when(pl.program_id(1) == 0)
    def _():
        c_ref[...] = jnp.zeros_like(c_ref)
        n_ref[...] = jnp.zeros_like(n_ref)
        m_ref[...] = jnp.zeros_like(m_ref)

    gates = gate_ref[...]
    gates_t = gates.T
    row = lax.broadcasted_iota(jnp.int32, (L, L), 0)
    col = lax.broadcasted_iota(jnp.int32, (L, L), 1)
    causal = col <= row

    for h in range(heads):
        q = q_ref[:, h * dk:(h + 1) * dk] * (dk ** -0.5)
        k = k_ref[:, h * dk:(h + 1) * dk]
        v = v_ref[:, h * dv:(h + 1) * dv]
        i_col = gates[:, h:h + 1]
        b_col = gates[:, heads + h:heads + h + 1]
        i_row = gates_t[h:h + 1, :]
        b_row = gates_t[heads + h:heads + h + 1, :]
        g = b_col[L - 1:L, :]
        m = m_ref[h:h + 1, 0:1]
        c_state = c_ref[h]
        n_state = n_ref[h:h + 1, :]

        d_log = jnp.where(causal, b_col - b_row + i_row, -jnp.inf)
        inter_log = b_col + m
        m_t = jnp.maximum(inter_log, jnp.max(d_log, axis=-1, keepdims=True))
        dmat = jnp.exp(d_log - m_t)
        inter_w = jnp.exp(inter_log - m_t)
        s = lax.dot_general(q, k, NT_DIMS, preferred_element_type=F32) * dmat
        num = (inter_w * jnp.dot(q, c_state.astype(BF16), preferred_element_type=F32)
               + jnp.dot(s.astype(BF16), v, preferred_element_type=F32))
        n_mxu = n_state.astype(BF16).astype(F32)
        qn = jnp.sum(q.astype(F32) * n_mxu, axis=-1, keepdims=True)
        den = inter_w * qn + jnp.sum(s, axis=-1, keepdims=True)
        h_out = num / jnp.maximum(jnp.abs(den), jnp.exp(-m_t))

        a_col = g - b_col + i_col
        m_new = jnp.maximum(g + m, jnp.max(a_col, axis=0, keepdims=True))
        decay = jnp.exp(g + m - m_new)
        kw = k.astype(F32) * jnp.exp(a_col - m_new)
        c_ref[h] = decay * c_state + lax.dot_general(
            kw.astype(BF16), v, (((0,), (0,)), ((), ())), preferred_element_type=F32)
        n_ref[h:h + 1, :] = decay * n_state + jnp.sum(kw, axis=0, keepdims=True)
        m_ref[h:h + 1, :] = jnp.broadcast_to(m_new, (1, LANES))

        hs = h_out * _rms_scale(h_out) * hg_ref[:, h * dv:(h + 1) * dv]
        og = jax.nn.sigmoid(o_ref[:, h * dv:(h + 1) * dv].astype(F32))
        out_ref[:, h * dv:(h + 1) * dv] = (hs * og).astype(out_ref.dtype)


def mlstm(proj, gates, head_g, *, batch, heads, dk, dv):
    t = proj.shape[0]
    nc = t // batch // ML_CHUNK
    qw, vw = heads * dk, heads * dv
    assert qw * 2 == vw
    row_blk = lambda b, c: b * nc + c
    return pl.pallas_call(
        functools.partial(_mlstm_kernel, heads=heads),
        grid=(batch, nc),
        in_specs=[pl.BlockSpec((ML_CHUNK, qw), lambda b, c: (row_blk(b, c), 0)),
                  pl.BlockSpec((ML_CHUNK, qw), lambda b, c: (row_blk(b, c), 1)),
                  pl.BlockSpec((ML_CHUNK, vw), lambda b, c: (row_blk(b, c), 1)),
                  pl.BlockSpec((ML_CHUNK, vw), lambda b, c: (row_blk(b, c), 2)),
                  pl.BlockSpec((ML_CHUNK, LANES), lambda b, c: (row_blk(b, c), 0)),
                  pl.BlockSpec((1, vw), lambda b, c: (0, 0))],
        out_specs=pl.BlockSpec((ML_CHUNK, vw), lambda b, c: (row_blk(b, c), 0)),
        out_shape=jax.ShapeDtypeStruct((t, vw), BF16),
        scratch_shapes=[pltpu.VMEM((heads, dk, dv), F32),
                        pltpu.VMEM((heads, dk), F32),
                        pltpu.VMEM((heads, LANES), F32)],
        compiler_params=_params(("arbitrary", "arbitrary")),
        name="mlstm",
    )(proj, proj, proj, proj, gates, head_g.reshape(1, vw))


NORM_ROWS = 256
IN_TM, IN_TN = 1024, 512
MM_TM, MM_TN = 1024, 1024
KS_TM, KS_TN, KS_TK = 1024, 1024, 4096


def _cast_rows(n_rows, n_steps):
    blocks = n_steps
    while n_rows % blocks or (n_rows // blocks) % BF16_SUBLANES:
        blocks -= 1
    return n_rows // blocks


def _ffn(h, g, w_up_bf16, w_down, layer, down_casts):
    t = h.shape[0]
    d_ff = w_up_bf16.shape[1]
    hn = rmsnorm(h, g, BF16, NORM_ROWS)
    up_steps = (t // MM_TM) * (d_ff // MM_TN)
    a, w_down_bf16 = matmul(
        hn, w_up_bf16, n_out=d_ff, tm=MM_TM, tn=MM_TN, epilogue=_sqrelu, out_dtype=BF16,
        name="ffn_up",
        casts=[SideCast(w_down, layer, d_ff, _cast_rows(d_ff, up_steps))])
    return matmul_ksplit_res(a, w_down_bf16, h, tm=KS_TM, tn=KS_TN, tk=KS_TK,
                             name="ffn_down", casts=down_casts)


def kernel(x, norm_mix, norm_ffn, gm_w_in, gm_ln_g, gm_ln_b, gm_w_s, gm_b_s, gm_w_out,
           ml_w_in, ml_b_gate, ml_head_g, ml_w_out, ffn_w_up, ffn_w_down, norm_final):
    batch, seq, d = x.shape
    t = batch * seq
    d_ff = ffn_w_up.shape[2]
    h = x.reshape(t, d)

    hn = rmsnorm(h, norm_mix[0], BF16, NORM_ROWS)
    n_z = gm_w_in.shape[2]
    in_steps = (t // IN_TM) * (n_z // IN_TN)
    z, gm_w_out_bf16, w_up0_bf16 = matmul(
        hn, gm_w_in, layer=0, n_out=n_z, tm=IN_TM, tn=IN_TN, epilogue=_gelu,
        out_dtype=BF16, name="gmlp_in",
        casts=[SideCast(gm_w_out, 0, d, _cast_rows(d, in_steps)),
               SideCast(ffn_w_up, 0, d, _cast_rows(d, in_steps))])
    gated = gmlp_gate(z, gm_ln_g[0], gm_ln_b[0], gm_w_s[0], gm_b_s[0])
    (h,) = matmul(gated, gm_w_out_bf16, n_out=d, tm=MM_TM, tn=MM_TN, epilogue=_identity,
                  out_dtype=F32, res=h, name="gmlp_out")

    heads = ML_HEADS
    dv = ml_head_g.shape[2]
    dk = dv // 2
    n_main = 2 * heads * dk + 2 * heads * dv
    ml_w_in_t = jnp.swapaxes(ml_w_in, 1, 2)
    down_steps = (t // KS_TM) * (d // KS_TN) * (d_ff // KS_TK)
    h, ml_w_in_t_bf16 = _ffn(
        h, norm_ffn[0], w_up0_bf16, ffn_w_down, 0,
        [SideCast(ml_w_in_t, 0, n_main, _cast_rows(n_main, down_steps))])

    hn, gates = norm_gates(h, norm_mix[1], ml_w_in_t[0, n_main:, :], ml_b_gate[0], NORM_ROWS)
    proj_steps = (t // MM_TM) * (n_main // MM_TN)
    proj, ml_w_out_bf16, w_up1_bf16 = matmul(
        hn, ml_w_in_t_bf16, n_out=n_main, tm=MM_TM, tn=MM_TN, epilogue=_identity,
        out_dtype=BF16, transposed_w=True, name="mlstm_in",
        casts=[SideCast(ml_w_out, 0, d, _cast_rows(d, proj_steps)),
               SideCast(ffn_w_up, 1, d, _cast_rows(d, proj_steps))])
    hs = mlstm(proj, gates, ml_head_g[0], batch=batch, heads=heads, dk=dk, dv=dv)
    (h,) = matmul(hs, ml_w_out_bf16, n_out=d, tm=MM_TM, tn=MM_TN, epilogue=_identity,
                  out_dtype=F32, res=h, name="mlstm_out")
    (h,) = _ffn(h, norm_ffn[1], w_up1_bf16, ffn_w_down, 1, [])

    out = rmsnorm(h, norm_final, F32, NORM_ROWS)
    return out.reshape(batch, seq, d)
```

```python
import functools
import math

import jax
import jax.numpy as jnp
from jax import lax
from jax.experimental import pallas as pl
from jax.experimental.pallas import tpu as pltpu

F32 = jnp.float32
BF16 = jnp.bfloat16

EPS = 1e-6
GM_CHUNK = 128
GM_HEADS = 8
ML_HEADS = 8
ML_CHUNK = 128
GATE_CAP = 15.0
LANES = 128
BF16_SUBLANES = 16

VMEM_LIMIT_BYTES = 63 * 1024 * 1024

NT_DIMS = (((1,), (1,)), ((), ()))


def _params(semantics):
    return pltpu.CompilerParams(
        dimension_semantics=semantics, vmem_limit_bytes=VMEM_LIMIT_BYTES)


class SideCast:
    def __init__(self, src, layer, n_rows, n_steps):
        blocks = n_steps
        while n_rows % blocks or (n_rows // blocks) % BF16_SUBLANES:
            blocks -= 1
        self.src, self.layer = src, layer
        self.n_blocks, self.block_rows = blocks, n_rows // blocks
        self.cols = src.shape[2]
        self.out_shape = jax.ShapeDtypeStruct((n_rows, self.cols), BF16)

    def specs(self, step_of):
        last, layer = self.n_blocks - 1, self.layer
        blk = lambda *g: jnp.minimum(step_of(*g), last)
        return (pl.BlockSpec((None, self.block_rows, self.cols),
                             lambda *g: (layer, blk(*g), 0)),
                pl.BlockSpec((self.block_rows, self.cols), lambda *g: (blk(*g), 0)))


def _do_side_casts(src_refs, dst_refs):
    for s_ref, d_ref in zip(src_refs, dst_refs):
        d_ref[...] = s_ref[...].astype(BF16)


def _add_casts(casts, step_of, in_specs, args, out_specs, out_shape):
    for c in casts:
        i_spec, o_spec = c.specs(step_of)
        in_specs.append(i_spec)
        args.append(c.src)
        out_specs.append(o_spec)
        out_shape.append(c.out_shape)


def _rms_scale(x):
    return lax.rsqrt(jnp.mean(x * x, axis=-1, keepdims=True) + EPS)


def _rmsnorm_kernel(x_ref, g_ref, *refs, n_cast):
    cast_src = refs[:n_cast]
    o_ref = refs[n_cast]
    cast_dst = refs[n_cast + 1:]
    x = x_ref[...]
    o_ref[...] = (x * _rms_scale(x) * g_ref[...]).astype(o_ref.dtype)
    _do_side_casts(cast_src, cast_dst)


def rmsnorm(x, g, out_dtype, rows, make_casts=None):
    t, d = x.shape
    n_steps = t // rows
    casts = make_casts(n_steps) if make_casts else []
    in_specs = [pl.BlockSpec((rows, d), lambda i: (i, 0)),
                pl.BlockSpec((1, d), lambda i: (0, 0))]
    args = [x, g.reshape(1, d)]
    out_specs = [pl.BlockSpec((rows, d), lambda i: (i, 0))]
    out_shape = [jax.ShapeDtypeStruct((t, d), out_dtype)]
    _add_casts(casts, lambda i: i, in_specs, args, out_specs, out_shape)
    return pl.pallas_call(
        functools.partial(_rmsnorm_kernel, n_cast=len(casts)),
        grid=(n_steps,),
        in_specs=in_specs,
        out_specs=out_specs,
        out_shape=out_shape,
        compiler_params=_params(("arbitrary",)),
        name="rmsnorm",
    )(*args)


def _row_scale(ss_ref, d):
    ss = ss_ref[...]
    part = sum(ss[:, c * LANES:(c + 1) * LANES] for c in range(ss.shape[1] // LANES))
    return lax.rsqrt(jnp.sum(part, axis=-1, keepdims=True) * (1.0 / d) + EPS)


def _lane_partial_sumsq(h):
    sq = h * h
    return sum(sq[:, c * LANES:(c + 1) * LANES] for c in range(h.shape[1] // LANES))


def _split3_bf16(x):
    hi = x.astype(BF16)
    r = x - hi.astype(F32)
    mid = r.astype(BF16)
    lo = (r - mid.astype(F32)).astype(BF16)
    return hi, mid, lo


def _gates_kernel(hg_ref, ss_ref, wg_ref, bg_ref, gate_ref, *, heads, d):
    pre = lax.dot_general(hg_ref[...], wg_ref[...], NT_DIMS, preferred_element_type=F32)
    pre = pre * _row_scale(ss_ref, d) + bg_ref[...]
    pre = GATE_CAP * jnp.tanh(pre / GATE_CAP)
    log_f = jnp.minimum(pre, 0.0) - jnp.log1p(jnp.exp(-jnp.abs(pre)))
    row = lax.broadcasted_iota(jnp.int32, (ML_CHUNK, ML_CHUNK), 0)
    col = lax.broadcasted_iota(jnp.int32, (ML_CHUNK, ML_CHUNK), 1)
    tril = (col <= row).astype(BF16)
    lane = lax.broadcasted_iota(jnp.int32, (ML_CHUNK, LANES), 1)
    for c in range(pre.shape[0] // ML_CHUNK):
        sl = slice(c * ML_CHUNK, (c + 1) * ML_CHUNK)
        cum = sum(jnp.dot(tril, p, preferred_element_type=F32)
                  for p in _split3_bf16(log_f[sl]))
        gate_ref[sl, :] = jnp.where(lane < heads, pre[sl], cum)


def mlstm_gates(hg, ss, w_gate_t, b_gate, rows):
    t, d = hg.shape
    n_gate = w_gate_t.shape[0]
    wg = jnp.zeros((LANES, d), BF16).at[:n_gate, :].set(w_gate_t.astype(BF16))
    bg = jnp.zeros((1, LANES), F32).at[0, :n_gate].set(b_gate)
    return pl.pallas_call(
        functools.partial(_gates_kernel, heads=n_gate // 2, d=d),
        grid=(t // rows,),
        in_specs=[pl.BlockSpec((rows, d), lambda i: (i, 0)),
                  pl.BlockSpec((rows, ss.shape[1]), lambda i: (i, 0)),
                  pl.BlockSpec((LANES, d), lambda i: (0, 0)),
                  pl.BlockSpec((1, LANES), lambda i: (0, 0))],
        out_specs=pl.BlockSpec((rows, LANES), lambda i: (i, 0)),
        out_shape=jax.ShapeDtypeStruct((t, LANES), F32),
        compiler_params=_params(("arbitrary",)),
        name="mlstm_gates",
    )(hg, ss, wg, bg)


def _gelu(x):
    return 0.5 * x * (1.0 + lax.erf(x * math.sqrt(0.5)))


def _sqrelu(x):
    r = jnp.maximum(x, 0.0)
    return r * r


def _identity(x):
    return x


def _emit_norm_inputs(h, g_ref, hg_ref, ss_ref):
    hg_ref[...] = (h * g_ref[...]).astype(BF16)
    ss_ref[...] = _lane_partial_sumsq(h)


def _mm_kernel(*refs, epilogue, has_res, has_scale, has_norm, n_cast, transposed_w, d_in):
    it = iter(refs)
    a_ref, w_ref = next(it), next(it)
    res_ref = next(it) if has_res else None
    ss_in_ref = next(it) if has_scale else None
    g_ref = next(it) if has_norm else None
    cast_src = [next(it) for _ in range(n_cast)]
    o_ref = next(it)
    hg_ref, ss_out_ref = (next(it), next(it)) if has_norm else (None, None)
    cast_dst = [next(it) for _ in range(n_cast)]

    if transposed_w:
        acc = lax.dot_general(a_ref[...], w_ref[...], NT_DIMS, preferred_element_type=F32)
    else:
        acc = jnp.dot(a_ref[...], w_ref[...], preferred_element_type=F32)
    if has_scale:
        acc = acc * _row_scale(ss_in_ref, d_in)
    out = epilogue(acc)
    if has_res:
        out = res_ref[...] + out
    o_ref[...] = out.astype(o_ref.dtype)
    if has_norm:
        _emit_norm_inputs(out, g_ref, hg_ref, ss_out_ref)
    _do_side_casts(cast_src, cast_dst)


def matmul(a, w, *, n_out, tm, tn, epilogue, out_dtype, name, res=None, row_ss=None,
           next_norm_g=None, transposed_w=False, make_casts=None):
    m, k = a.shape
    assert n_out % tn == 0 and m % tm == 0
    ni, nj = m // tm, n_out // tn
    casts = make_casts(ni * nj) if make_casts else []
    if transposed_w:
        assert w.shape[1] == k
        w_spec = pl.BlockSpec((tn, k), lambda j, i: (j, 0))
    else:
        assert w.shape[0] == k
        w_spec = pl.BlockSpec((k, tn), lambda j, i: (0, j))
    in_specs = [pl.BlockSpec((tm, k), lambda j, i: (i, 0)), w_spec]
    args = [a, w]
    tile = pl.BlockSpec((tm, tn), lambda j, i: (i, j))
    if res is not None:
        in_specs.append(tile)
        args.append(res)
    if row_ss is not None:
        in_specs.append(pl.BlockSpec((tm, row_ss.shape[1]), lambda j, i: (i, 0)))
        args.append(row_ss)
    out_specs = [tile]
    out_shape = [jax.ShapeDtypeStruct((m, n_out), out_dtype)]
    if next_norm_g is not None:
        in_specs.append(pl.BlockSpec((1, tn), lambda j, i: (0, j)))
        args.append(next_norm_g.reshape(1, n_out))
        out_specs += [tile, pl.BlockSpec((tm, LANES), lambda j, i: (i, j))]
        out_shape += [jax.ShapeDtypeStruct((m, n_out), BF16),
                      jax.ShapeDtypeStruct((m, nj * LANES), F32)]
    _add_casts(casts, lambda j, i: j * ni + i, in_specs, args, out_specs, out_shape)
    return pl.pallas_call(
        functools.partial(_mm_kernel, epilogue=epilogue, has_res=res is not None,
                          has_scale=row_ss is not None, has_norm=next_norm_g is not None,
                          n_cast=len(casts), transposed_w=transposed_w, d_in=k),
        grid=(nj, ni),
        in_specs=in_specs,
        out_specs=out_specs,
        out_shape=out_shape,
        compiler_params=_params(("arbitrary", "arbitrary")),
        name=name,
    )(*args)


def _mm_ksplit_kernel(*refs, has_norm, n_cast):
    it = iter(refs)
    a_ref, w_ref, res_ref = next(it), next(it), next(it)
    g_ref = next(it) if has_norm else None
    cast_src = [next(it) for _ in range(n_cast)]
    o_ref = next(it)
    hg_ref, ss_out_ref = (next(it), next(it)) if has_norm else (None, None)
    cast_dst = [next(it) for _ in range(n_cast)]

    @pl.when(pl.program_id(2) == 0)
    def _():
        o_ref[...] = res_ref[...]

    out = o_ref[...] + jnp.dot(a_ref[...], w_ref[...], preferred_element_type=F32)
    o_ref[...] = out
    if has_norm:
        _emit_norm_inputs(out, g_ref, hg_ref, ss_out_ref)
    _do_side_casts(cast_src, cast_dst)


def matmul_ksplit_res(a, w, res, *, tm, tn, tk, name, next_norm_g=None, make_casts=None):
    m, k = a.shape
    n = w.shape[1]
    assert w.shape[0] == k and m % tm == 0 and n % tn == 0 and k % tk == 0
    nj, ni, nk = n // tn, m // tm, k // tk
    casts = make_casts(nj * ni * nk) if make_casts else []
    tile = pl.BlockSpec((tm, tn), lambda j, i, kk: (i, j))
    in_specs = [pl.BlockSpec((tm, tk), lambda j, i, kk: (i, kk)),
                pl.BlockSpec((tk, tn), lambda j, i, kk: (kk, j)),
                tile]
    args = [a, w, res]
    out_specs = [tile]
    out_shape = [jax.ShapeDtypeStruct((m, n), F32)]
    if next_norm_g is not None:
        in_specs.append(pl.BlockSpec((1, tn), lambda j, i, kk: (0, j)))
        args.append(next_norm_g.reshape(1, n))
        out_specs += [tile, pl.BlockSpec((tm, LANES), lambda j, i, kk: (i, j))]
        out_shape += [jax.ShapeDtypeStruct((m, n), BF16),
                      jax.ShapeDtypeStruct((m, nj * LANES), F32)]
    _add_casts(casts, lambda j, i, kk: (j * ni + i) * nk + kk,
               in_specs, args, out_specs, out_shape)
    return pl.pallas_call(
        functools.partial(_mm_ksplit_kernel, has_norm=next_norm_g is not None,
                          n_cast=len(casts)),
        grid=(nj, ni, nk),
        in_specs=in_specs,
        out_specs=out_specs,
        out_shape=out_shape,
        compiler_params=_params(("arbitrary", "arbitrary", "arbitrary")),
        name=name,
    )(*args)


def _gmlp_gate_kernel(u_ref, v_ref, lng_ref, lnb_ref, ws_ref, bs_ref, o_ref):
    v = v_ref[...].astype(F32)
    mu = jnp.mean(v, axis=-1, keepdims=True)
    vc = v - mu
    vn = vc * lax.rsqrt(jnp.mean(vc * vc, axis=-1, keepdims=True) + EPS)
    vn = (vn * lng_ref[...] + lnb_ref[...]).astype(BF16)
    row = lax.broadcasted_iota(jnp.int32, (GM_CHUNK, GM_CHUNK), 0)
    col = lax.broadcasted_iota(jnp.int32, (GM_CHUNK, GM_CHUNK), 1)
    causal = col <= row
    hd = v.shape[1] // GM_HEADS
    for h in range(GM_HEADS):
        sl = slice(h * hd, (h + 1) * hd)
        w = jnp.where(causal, ws_ref[h], 0.0).astype(BF16)
        mixed = jnp.dot(w, vn[:, sl], preferred_element_type=F32) + bs_ref[:, h:h + 1]
        o_ref[:, sl] = (u_ref[:, sl].astype(F32) * mixed).astype(o_ref.dtype)


def gmlp_gate(z, ln_g, ln_b, w_s, b_s):
    t, two_w = z.shape
    width = two_w // 2
    return pl.pallas_call(
        _gmlp_gate_kernel,
        grid=(t // GM_CHUNK,),
        in_specs=[pl.BlockSpec((GM_CHUNK, width), lambda i: (i, 0)),
                  pl.BlockSpec((GM_CHUNK, width), lambda i: (i, 1)),
                  pl.BlockSpec((1, width), lambda i: (0, 0)),
                  pl.BlockSpec((1, width), lambda i: (0, 0)),
                  pl.BlockSpec((GM_HEADS, GM_CHUNK, GM_CHUNK), lambda i: (0, 0, 0)),
                  pl.BlockSpec((GM_CHUNK, GM_HEADS), lambda i: (0, 0))],
        out_specs=pl.BlockSpec((GM_CHUNK, width), lambda i: (i, 0)),
        out_shape=jax.ShapeDtypeStruct((t, width), BF16),
        compiler_params=_params(("arbitrary",)),
        name="gmlp_gate",
    )(z, z, ln_g.reshape(1, width), ln_b.reshape(1, width), w_s, b_s.T)


def _mlstm_kernel(q_ref, k_ref, v_ref, o_ref, gate_ref, hg_ref, out_ref,
                  c_ref, n_ref, m_ref, *, heads):
    dk = q_ref.shape[1] // heads
    dv = v_ref.shape[1] // heads
    L = ML_CHUNK

    @pl.when(pl.program_id(1) == 0)
    def _():
        c_ref[...] = jnp.zeros_like(c_ref)
        n_ref[...] = jnp.zeros_like(n_ref)
        m_ref[...] = jnp.zeros_like(m_ref)

    gates = gate_ref[...]
    gates_t = gates.T
    row = lax.broadcasted_iota(jnp.int32, (L, L), 0)
    col = lax.broadcasted_iota(jnp.int32, (L, L), 1)
    causal = col <= row

    for h in range(heads):
        q = q_ref[:, h * dk:(h + 1) * dk] * (dk ** -0.5)
        k = k_ref[:, h * dk:(h + 1) * dk]
        v = v_ref[:, h * dv:(h + 1) * dv]
        i_col = gates[:, h:h + 1]
        b_col = gates[:, heads + h:heads + h + 1]
        i_row = gates_t[h:h + 1, :]
        b_row = gates_t[heads + h:heads + h + 1, :]
        g = b_col[L - 1:L, :]
        m = m_ref[h:h + 1, 0:1]
        c_state = c_ref[h]
        n_state = n_ref[h:h + 1, :]

        d_log = jnp.where(causal, b_col - b_row + i_row, -jnp.inf)
        inter_log = b_col + m
        m_t = jnp.maximum(inter_log, jnp.max(d_log, axis=-1, keepdims=True))
        dmat = jnp.exp(d_log - m_t)
        inter_w = jnp.exp(inter_log - m_t)
        s = lax.dot_general(q, k, NT_DIMS, preferred_element_type=F32) * dmat
        num = (inter_w * jnp.dot(q, c_state.astype(BF16), preferred_element_type=F32)
               + jnp.dot(s.astype(BF16), v, preferred_element_type=F32))
        n_mxu = n_state.astype(BF16).astype(F32)
        qn = jnp.sum(q.astype(F32) * n_mxu, axis=-1, keepdims=True)
        den = inter_w * qn + jnp.sum(s, axis=-1, keepdims=True)
        h_out = num / jnp.maximum(jnp.abs(den), jnp.exp(-m_t))

        a_col = g - b_col + i_col
        m_new = jnp.maximum(g + m, jnp.max(a_col, axis=0, keepdims=True))
        decay = jnp.exp(g + m - m_new)
        kw = k.astype(F32) * jnp.exp(a_col - m_new)
        c_ref[h] = decay * c_state + lax.dot_general(
            kw.astype(BF16), v, (((0,), (0,)), ((), ())), preferred_element_type=F32)
        n_ref[h:h + 1, :] = decay * n_state + jnp.sum(kw, axis=0, keepdims=True)
        m_ref[h:h + 1, :] = jnp.broadcast_to(m_new, (1, LANES))

        hs = h_out * _rms_scale(h_out) * hg_ref[:, h * dv:(h + 1) * dv]
        og = jax.nn.sigmoid(o_ref[:, h * dv:(h + 1) * dv].astype(F32))
        out_ref[:, h * dv:(h + 1) * dv] = (hs * og).astype(out_ref.dtype)


def mlstm(proj, gates, head_g, *, batch, heads, dk, dv):
    t = proj.shape[0]
    nc = t // batch // ML_CHUNK
    qw, vw = heads * dk, heads * dv
    assert qw * 2 == vw
    row_blk = lambda b, c: b * nc + c
    return pl.pallas_call(
        functools.partial(_mlstm_kernel, heads=heads),
        grid=(batch, nc),
        in_specs=[pl.BlockSpec((ML_CHUNK, qw), lambda b, c: (row_blk(b, c), 0)),
                  pl.BlockSpec((ML_CHUNK, qw), lambda b, c: (row_blk(b, c), 1)),
                  pl.BlockSpec((ML_CHUNK, vw), lambda b, c: (row_blk(b, c), 1)),
                  pl.BlockSpec((ML_CHUNK, vw), lambda b, c: (row_blk(b, c), 2)),
                  pl.BlockSpec((ML_CHUNK, LANES), lambda b, c: (row_blk(b, c), 0)),
                  pl.BlockSpec((1, vw), lambda b, c: (0, 0))],
        out_specs=pl.BlockSpec((ML_CHUNK, vw), lambda b, c: (row_blk(b, c), 0)),
        out_shape=jax.ShapeDtypeStruct((t, vw), BF16),
        scratch_shapes=[pltpu.VMEM((heads, dk, dv), F32),
                        pltpu.VMEM((heads, dk), F32),
                        pltpu.VMEM((heads, LANES), F32)],
        compiler_params=_params(("arbitrary", "arbitrary")),
        name="mlstm",
    )(proj, proj, proj, proj, gates, head_g.reshape(1, vw))


NORM_ROWS = 256
GATE_ROWS = 512
MM_TM, MM_TN = 1024, 1024
KS_TM, KS_TN, KS_TK = 1024, 1024, 4096


def _ffn(h, hg, ss, w_up_bf16, w_down, layer, next_norm_g, extra_casts):
    d_ff = w_up_bf16.shape[1]
    a, w_down_bf16, *extra = matmul(
        hg, w_up_bf16, n_out=d_ff, tm=MM_TM, tn=MM_TN, epilogue=_sqrelu, out_dtype=BF16,
        row_ss=ss, name="ffn_up",
        make_casts=lambda n: [SideCast(w_down, layer, d_ff, n)] + extra_casts(n))
    outs = matmul_ksplit_res(a, w_down_bf16, h, tm=KS_TM, tn=KS_TN, tk=KS_TK,
                             name="ffn_down", next_norm_g=next_norm_g)
    return outs, extra


def kernel(x, norm_mix, norm_ffn, gm_w_in, gm_ln_g, gm_ln_b, gm_w_s, gm_b_s, gm_w_out,
           ml_w_in, ml_b_gate, ml_head_g, ml_w_out, ffn_w_up, ffn_w_down, norm_final):
    batch, seq, d = x.shape
    h = x.reshape(batch * seq, d)
    mm = functools.partial(matmul, tm=MM_TM, tn=MM_TN)

    hn, gm_w_in_bf16 = rmsnorm(
        h, norm_mix[0], BF16, NORM_ROWS,
        make_casts=lambda n: [SideCast(gm_w_in, 0, d, n)])
    z, gm_w_out_bf16, w_up0_bf16 = mm(
        hn, gm_w_in_bf16, n_out=gm_w_in.shape[2], epilogue=_gelu, out_dtype=BF16,
        name="gmlp_in",
        make_casts=lambda n: [SideCast(gm_w_out, 0, d, n), SideCast(ffn_w_up, 0, d, n)])
    gated = gmlp_gate(z, gm_ln_g[0], gm_ln_b[0], gm_w_s[0], gm_b_s[0])
    h, hg, ss = mm(gated, gm_w_out_bf16, n_out=d, epilogue=_identity, out_dtype=F32,
                   res=h, next_norm_g=norm_ffn[0], name="gmlp_out")

    heads = ML_HEADS
    dv = ml_head_g.shape[2]
    dk = dv // 2
    n_main = 2 * heads * dk + 2 * heads * dv
    ml_w_in_t = jnp.swapaxes(ml_w_in, 1, 2)
    (h, hg, ss), (ml_w_in_t_bf16,) = _ffn(
        h, hg, ss, w_up0_bf16, ffn_w_down, 0, norm_mix[1],
        lambda n: [SideCast(ml_w_in_t, 0, n_main, n)])

    gates = mlstm_gates(hg, ss, ml_w_in_t[0, n_main:, :], ml_b_gate[0], GATE_ROWS)
    proj, ml_w_out_bf16, w_up1_bf16 = mm(
        hg, ml_w_in_t_bf16, n_out=n_main, epilogue=_identity, out_dtype=BF16,
        row_ss=ss, transposed_w=True, name="mlstm_in",
        make_casts=lambda n: [SideCast(ml_w_out, 0, d, n), SideCast(ffn_w_up, 1, d, n)])
    hs = mlstm(proj, gates, ml_head_g[0], batch=batch, heads=heads, dk=dk, dv=dv)
    h, hg, ss = mm(hs, ml_w_out_bf16, n_out=d, epilogue=_identity, out_dtype=F32,
                   res=h, next_norm_g=norm_ffn[1], name="mlstm_out")
    (h,), _ = _ffn(h, hg, ss, w_up1_bf16, ffn_w_down, 1, None, lambda n: [])

    (out,) = rmsnorm(h, norm_final, F32, NORM_ROWS)
    return out.reshape(batch, seq, d)
```

```python
import functools
import math

import jax
import jax.numpy as jnp
from jax import lax
from jax.experimental import pallas as pl
from jax.experimental.pallas import tpu as pltpu

F32 = jnp.float32
BF16 = jnp.bfloat16

EPS = 1e-6
GM_CHUNK = 128
GM_HEADS = 8
ML_HEADS = 8
ML_CHUNK = 128
GATE_CAP = 15.0
LANES = 128
BF16_SUBLANES = 16

VMEM_LIMIT_BYTES = 63 * 1024 * 1024

NT_DIMS = (((1,), (1,)), ((), ()))


def _params(semantics):
    return pltpu.CompilerParams(
        dimension_semantics=semantics, vmem_limit_bytes=VMEM_LIMIT_BYTES)


class SideCast:
    def __init__(self, src, layer, n_rows, n_steps):
        blocks = n_steps
        while n_rows % blocks or (n_rows // blocks) % BF16_SUBLANES:
            blocks -= 1
        self.src, self.layer = src, layer
        self.n_blocks, self.block_rows = blocks, n_rows // blocks
        self.cols = src.shape[2]
        self.out_shape = jax.ShapeDtypeStruct((n_rows, self.cols), BF16)

    def specs(self, step_of):
        last, layer = self.n_blocks - 1, self.layer
        blk = lambda *g: jnp.minimum(step_of(*g), last)
        return (pl.BlockSpec((None, self.block_rows, self.cols),
                             lambda *g: (layer, blk(*g), 0)),
                pl.BlockSpec((self.block_rows, self.cols), lambda *g: (blk(*g), 0)))


def _do_side_casts(src_refs, dst_refs):
    for s_ref, d_ref in zip(src_refs, dst_refs):
        d_ref[...] = s_ref[...].astype(BF16)


def _add_casts(casts, step_of, in_specs, args, out_specs, out_shape):
    for c in casts:
        i_spec, o_spec = c.specs(step_of)
        in_specs.append(i_spec)
        args.append(c.src)
        out_specs.append(o_spec)
        out_shape.append(c.out_shape)


def _rms_scale(x):
    return lax.rsqrt(jnp.mean(x * x, axis=-1, keepdims=True) + EPS)


def _rmsnorm_kernel(x_ref, g_ref, *refs, n_cast):
    cast_src = refs[:n_cast]
    o_ref = refs[n_cast]
    cast_dst = refs[n_cast + 1:]
    x = x_ref[...]
    o_ref[...] = (x * _rms_scale(x) * g_ref[...]).astype(o_ref.dtype)
    _do_side_casts(cast_src, cast_dst)


def rmsnorm(x, g, out_dtype, rows, make_casts=None):
    t, d = x.shape
    n_steps = t // rows
    casts = make_casts(n_steps) if make_casts else []
    in_specs = [pl.BlockSpec((rows, d), lambda i: (i, 0)),
                pl.BlockSpec((1, d), lambda i: (0, 0))]
    args = [x, g.reshape(1, d)]
    out_specs = [pl.BlockSpec((rows, d), lambda i: (i, 0))]
    out_shape = [jax.ShapeDtypeStruct((t, d), out_dtype)]
    _add_casts(casts, lambda i: i, in_specs, args, out_specs, out_shape)
    return pl.pallas_call(
        functools.partial(_rmsnorm_kernel, n_cast=len(casts)),
        grid=(n_steps,),
        in_specs=in_specs,
        out_specs=out_specs,
        out_shape=out_shape,
        compiler_params=_params(("arbitrary",)),
        name="rmsnorm",
    )(*args)


def _row_scale(ss_ref, d):
    ss = ss_ref[...]
    part = sum(ss[:, c * LANES:(c + 1) * LANES] for c in range(ss.shape[1] // LANES))
    return lax.rsqrt(jnp.sum(part, axis=-1, keepdims=True) * (1.0 / d) + EPS)


def _lane_partial_sumsq(h):
    sq = h * h
    return sum(sq[:, c * LANES:(c + 1) * LANES] for c in range(h.shape[1] // LANES))


def _split3_bf16(x):
    hi = x.astype(BF16)
    r = x - hi.astype(F32)
    mid = r.astype(BF16)
    lo = (r - mid.astype(F32)).astype(BF16)
    return hi, mid, lo


def _gates_kernel(hg_ref, ss_ref, wg_ref, bg_ref, gate_ref, *, heads, d):
    pre = lax.dot_general(hg_ref[...], wg_ref[...], NT_DIMS, preferred_element_type=F32)
    pre = pre * _row_scale(ss_ref, d) + bg_ref[...]
    pre = GATE_CAP * jnp.tanh(pre / GATE_CAP)
    log_f = jnp.minimum(pre, 0.0) - jnp.log1p(jnp.exp(-jnp.abs(pre)))
    row = lax.broadcasted_iota(jnp.int32, (ML_CHUNK, ML_CHUNK), 0)
    col = lax.broadcasted_iota(jnp.int32, (ML_CHUNK, ML_CHUNK), 1)
    tril = (col <= row).astype(BF16)
    lane = lax.broadcasted_iota(jnp.int32, (ML_CHUNK, LANES), 1)
    for c in range(pre.shape[0] // ML_CHUNK):
        sl = slice(c * ML_CHUNK, (c + 1) * ML_CHUNK)
        cum = sum(jnp.dot(tril, p, preferred_element_type=F32)
                  for p in _split3_bf16(log_f[sl]))
        gate_ref[sl, :] = jnp.where(lane < heads, pre[sl], cum)


def mlstm_gates(hg, ss, w_gate_t, b_gate, rows):
    t, d = hg.shape
    n_gate = w_gate_t.shape[0]
    wg = jnp.zeros((LANES, d), BF16).at[:n_gate, :].set(w_gate_t.astype(BF16))
    bg = jnp.zeros((1, LANES), F32).at[0, :n_gate].set(b_gate)
    return pl.pallas_call(
        functools.partial(_gates_kernel, heads=n_gate // 2, d=d),
        grid=(t // rows,),
        in_specs=[pl.BlockSpec((rows, d), lambda i: (i, 0)),
                  pl.BlockSpec((rows, ss.shape[1]), lambda i: (i, 0)),
                  pl.BlockSpec((LANES, d), lambda i: (0, 0)),
                  pl.BlockSpec((1, LANES), lambda i: (0, 0))],
        out_specs=pl.BlockSpec((rows, LANES), lambda i: (i, 0)),
        out_shape=jax.ShapeDtypeStruct((t, LANES), F32),
        compiler_params=_params(("arbitrary",)),
        name="mlstm_gates",
    )(hg, ss, wg, bg)


def _gelu(x):
    return 0.5 * x * (1.0 + lax.erf(x * math.sqrt(0.5)))


def _sqrelu(x):
    r = jnp.maximum(x, 0.0)
    return r * r


def _identity(x):
    return x


def _emit_norm_inputs(h, g_ref, hg_ref, ss_ref):
    hg_ref[...] = (h * g_ref[...]).astype(BF16)
    ss_ref[...] = _lane_partial_sumsq(h)


def _mm_kernel(*refs, epilogue, has_res, has_scale, has_norm, n_cast, transposed_w, d_in):
    it = iter(refs)
    a_ref, w_ref = next(it), next(it)
    res_ref = next(it) if has_res else None
    ss_in_ref = next(it) if has_scale else None
    g_ref = next(it) if has_norm else None
    cast_src = [next(it) for _ in range(n_cast)]
    o_ref = next(it)
    hg_ref, ss_out_ref = (next(it), next(it)) if has_norm else (None, None)
    cast_dst = [next(it) for _ in range(n_cast)]

    if transposed_w:
        acc = lax.dot_general(a_ref[...], w_ref[...], NT_DIMS, preferred_element_type=F32)
    else:
        acc = jnp.dot(a_ref[...], w_ref[...], preferred_element_type=F32)
    if has_scale:
        acc = acc * _row_scale(ss_in_ref, d_in)
    out = epilogue(acc)
    if has_res:
        out = res_ref[...] + out
    o_ref[...] = out.astype(o_ref.dtype)
    if has_norm:
        _emit_norm_inputs(out, g_ref, hg_ref, ss_out_ref)
    _do_side_casts(cast_src, cast_dst)


def matmul(a, w, *, n_out, tm, tn, epilogue, out_dtype, name, res=None, row_ss=None,
           next_norm_g=None, transposed_w=False, make_casts=None):
    m, k = a.shape
    assert n_out % tn == 0 and m % tm == 0
    ni, nj = m // tm, n_out // tn
    casts = make_casts(ni * nj) if make_casts else []
    if transposed_w:
        assert w.shape[1] == k
        w_spec = pl.BlockSpec((tn, k), lambda j, i: (j, 0))
    else:
        assert w.shape[0] == k
        w_spec = pl.BlockSpec((k, tn), lambda j, i: (0, j))
    in_specs = [pl.BlockSpec((tm, k), lambda j, i: (i, 0)), w_spec]
    args = [a, w]
    tile = pl.BlockSpec((tm, tn), lambda j, i: (i, j))
    if res is not None:
        in_specs.append(tile)
        args.append(res)
    if row_ss is not None:
        in_specs.append(pl.BlockSpec((tm, row_ss.shape[1]), lambda j, i: (i, 0)))
        args.append(row_ss)
    out_specs = [tile]
    out_shape = [jax.ShapeDtypeStruct((m, n_out), out_dtype)]
    if next_norm_g is not None:
        in_specs.append(pl.BlockSpec((1, tn), lambda j, i: (0, j)))
        args.append(next_norm_g.reshape(1, n_out))
        out_specs += [tile, pl.BlockSpec((tm, LANES), lambda j, i: (i, j))]
        out_shape += [jax.ShapeDtypeStruct((m, n_out), BF16),
                      jax.ShapeDtypeStruct((m, nj * LANES), F32)]
    _add_casts(casts, lambda j, i: j * ni + i, in_specs, args, out_specs, out_shape)
    return pl.pallas_call(
        functools.partial(_mm_kernel, epilogue=epilogue, has_res=res is not None,
                          has_scale=row_ss is not None, has_norm=next_norm_g is not None,
                          n_cast=len(casts), transposed_w=transposed_w, d_in=k),
        grid=(nj, ni),
        in_specs=in_specs,
        out_specs=out_specs,
        out_shape=out_shape,
        compiler_params=_params(("arbitrary", "arbitrary")),
        name=name,
    )(*args)


def _mm_ksplit_kernel(*refs, has_norm, n_cast):
    it = iter(refs)
    a_ref, w_ref, res_ref = next(it), next(it), next(it)
    g_ref = next(it) if has_norm else None
    cast_src = [next(it) for _ in range(n_cast)]
    o_ref = next(it)
    hg_ref, ss_out_ref = (next(it), next(it)) if has_norm else (None, None)
    cast_dst = [next(it) for _ in range(n_cast)]

    @pl.when(pl.program_id(2) == 0)
    def _():
        o_ref[...] = res_ref[...]

    out = o_ref[...] + jnp.dot(a_ref[...], w_ref[...], preferred_element_type=F32)
    o_ref[...] = out
    if has_norm:
        _emit_norm_inputs(out, g_ref, hg_ref, ss_out_ref)
    _do_side_casts(cast_src, cast_dst)


def matmul_ksplit_res(a, w, res, *, tm, tn, tk, name, next_norm_g=None, make_casts=None):
    m, k = a.shape
    n = w.shape[1]
    assert w.shape[0] == k and m % tm == 0 and n % tn == 0 and k % tk == 0
    nj, ni, nk = n // tn, m // tm, k // tk
    casts = make_casts(nj * ni * nk) if make_casts else []
    tile = pl.BlockSpec((tm, tn), lambda j, i, kk: (i, j))
    in_specs = [pl.BlockSpec((tm, tk), lambda j, i, kk: (i, kk)),
                pl.BlockSpec((tk, tn), lambda j, i, kk: (kk, j)),
                tile]
    args = [a, w, res]
    out_specs = [tile]
    out_shape = [jax.ShapeDtypeStruct((m, n), F32)]
    if next_norm_g is not None:
        in_specs.append(pl.BlockSpec((1, tn), lambda j, i, kk: (0, j)))
        args.append(next_norm_g.reshape(1, n))
        out_specs += [tile, pl.BlockSpec((tm, LANES), lambda j, i, kk: (i, j))]
        out_shape += [jax.ShapeDtypeStruct((m, n), BF16),
                      jax.ShapeDtypeStruct((m, nj * LANES), F32)]
    _add_casts(casts, lambda j, i, kk: (j * ni + i) * nk + kk,
               in_specs, args, out_specs, out_shape)
    return pl.pallas_call(
        functools.partial(_mm_ksplit_kernel, has_norm=next_norm_g is not None,
                          n_cast=len(casts)),
        grid=(nj, ni, nk),
        in_specs=in_specs,
        out_specs=out_specs,
        out_shape=out_shape,
        compiler_params=_params(("arbitrary", "arbitrary", "arbitrary")),
        name=name,
    )(*args)


def _gmlp_gate_kernel(u_ref, v_ref, lng_ref, lnb_ref, ws_ref, bs_ref, o_ref):
    v = v_ref[...].astype(F32)
    mu = jnp.mean(v, axis=-1, keepdims=True)
    vc = v - mu
    vn = vc * lax.rsqrt(jnp.mean(vc * vc, axis=-1, keepdims=True) + EPS)
    vn = (vn * lng_ref[...] + lnb_ref[...]).astype(BF16)
    row = lax.broadcasted_iota(jnp.int32, (GM_CHUNK, GM_CHUNK), 0)
    col = lax.broadcasted_iota(jnp.int32, (GM_CHUNK, GM_CHUNK), 1)
    causal = col <= row
    hd = v.shape[1] // GM_HEADS
    for h in range(GM_HEADS):
        sl = slice(h * hd, (h + 1) * hd)
        w = jnp.where(causal, ws_ref[h], 0.0).astype(BF16)
        mixed = jnp.dot(w, vn[:, sl], preferred_element_type=F32) + bs_ref[:, h:h + 1]
        o_ref[:, sl] = (u_ref[:, sl].astype(F32) * mixed).astype(o_ref.dtype)


def gmlp_gate(z, ln_g, ln_b, w_s, b_s):
    t, two_w = z.shape
    width = two_w // 2
    return pl.pallas_call(
        _gmlp_gate_kernel,
        grid=(t // GM_CHUNK,),
        in_specs=[pl.BlockSpec((GM_CHUNK, width), lambda i: (i, 0)),
                  pl.BlockSpec((GM_CHUNK, width), lambda i: (i, 1)),
                  pl.BlockSpec((1, width), lambda i: (0, 0)),
                  pl.BlockSpec((1, width), lambda i: (0, 0)),
                  pl.BlockSpec((GM_HEADS, GM_CHUNK, GM_CHUNK), lambda i: (0, 0, 0)),
                  pl.BlockSpec((GM_CHUNK, GM_HEADS), lambda i: (0, 0))],
        out_specs=pl.BlockSpec((GM_CHUNK, width), lambda i: (i, 0)),
        out_shape=jax.ShapeDtypeStruct((t, width), BF16),
        compiler_params=_params(("arbitrary",)),
        name="gmlp_gate",
    )(z, z, ln_g.reshape(1, width), ln_b.reshape(1, width), w_s, b_s.T)


def _mlstm_kernel(q_ref, k_ref, v_ref, o_ref, gate_ref, hg_ref, out_ref,
                  c_ref, n_ref, m_ref, hbuf_ref, *, heads):
    dk = q_ref.shape[1] // heads
    dv = v_ref.shape[1] // heads
    L = ML_CHUNK

    @pl.when(pl.program_id(1) == 0)
    def _():
        c_ref[...] = jnp.zeros_like(c_ref)
        n_ref[...] = jnp.zeros_like(n_ref)
        m_ref[...] = jnp.zeros_like(m_ref)

    gates = gate_ref[...]
    gates_t = gates.T
    row = lax.broadcasted_iota(jnp.int32, (L, L), 0)
    col = lax.broadcasted_iota(jnp.int32, (L, L), 1)
    causal = col <= row

    H = range(heads)
    qs = [q_ref[:, h * dk:(h + 1) * dk] * (dk ** -0.5) for h in H]
    ks = [k_ref[:, h * dk:(h + 1) * dk] for h in H]
    i_col = [gates[:, h:h + 1] for h in H]
    b_col = [gates[:, heads + h:heads + h + 1] for h in H]
    i_row = [gates_t[h:h + 1, :] for h in H]
    b_row = [gates_t[heads + h:heads + h + 1, :] for h in H]
    g = [b_col[h][L - 1:L, :] for h in H]
    m = [m_ref[h:h + 1, 0:1] for h in H]

    d_log = [jnp.where(causal, b_col[h] - b_row[h] + i_row[h], -jnp.inf) for h in H]
    inter_log = [b_col[h] + m[h] for h in H]
    m_t = [jnp.maximum(inter_log[h], jnp.max(d_log[h], axis=-1, keepdims=True)) for h in H]
    dmat = [jnp.exp(d_log[h] - m_t[h]) for h in H]
    inter_w = [jnp.exp(inter_log[h] - m_t[h]) for h in H]
    a_col = [g[h] - b_col[h] + i_col[h] for h in H]
    m_new = [jnp.maximum(g[h] + m[h], jnp.max(a_col[h], axis=0, keepdims=True)) for h in H]
    decay = [jnp.exp(g[h] + m[h] - m_new[h]) for h in H]
    kscale = [jnp.exp(a_col[h] - m_new[h]) for h in H]

    s = [lax.dot_general(qs[h], ks[h], NT_DIMS, preferred_element_type=F32) * dmat[h]
         for h in H]
    s_sum = [jnp.sum(s[h], axis=-1, keepdims=True) for h in H]

    n_old = [n_ref[h:h + 1, :] for h in H]
    r_den = []
    for h in H:
        n_mxu = n_old[h].astype(BF16).astype(F32)
        qn = jnp.sum(qs[h].astype(F32) * n_mxu, axis=-1, keepdims=True)
        den = inter_w[h] * qn + s_sum[h]
        r_den.append(1.0 / jnp.maximum(jnp.abs(den), jnp.exp(-m_t[h])))

    def mxu_stage(h):
        kw = ks[h].astype(F32) * kscale[h]
        kw_mxu = kw.astype(BF16)
        s_mxu = s[h].astype(BF16)
        v = v_ref[:, h * dv:(h + 1) * dv]
        num = (inter_w[h] * jnp.dot(qs[h], c_ref[h].astype(BF16), preferred_element_type=F32)
               + jnp.dot(s_mxu, v, preferred_element_type=F32))
        hbuf_ref[:, h * dv:(h + 1) * dv] = num * r_den[h]
        c_ref[h] = decay[h] * c_ref[h] + lax.dot_general(
            kw_mxu, v, (((0,), (0,)), ((), ())), preferred_element_type=F32)
        n_ref[h:h + 1, :] = decay[h] * n_old[h] + jnp.sum(kw, axis=0, keepdims=True)
        m_ref[h:h + 1, :] = jnp.broadcast_to(m_new[h], (1, LANES))

    def norm_stage(h):
        h_out = hbuf_ref[:, h * dv:(h + 1) * dv]
        hs = h_out * _rms_scale(h_out) * hg_ref[:, h * dv:(h + 1) * dv]
        og = 0.5 * jnp.tanh(0.5 * o_ref[:, h * dv:(h + 1) * dv].astype(F32)) + 0.5
        out_ref[:, h * dv:(h + 1) * dv] = (hs * og).astype(out_ref.dtype)

    for h in range(heads + 1):
        if h < heads:
            mxu_stage(h)
        if h > 0:
            norm_stage(h - 1)


def mlstm(proj, gates, head_g, *, batch, heads, dk, dv):
    t = proj.shape[0]
    nc = t // batch // ML_CHUNK
    qw, vw = heads * dk, heads * dv
    assert qw * 2 == vw
    row_blk = lambda b, c: b * nc + c
    return pl.pallas_call(
        functools.partial(_mlstm_kernel, heads=heads),
        grid=(batch, nc),
        in_specs=[pl.BlockSpec((ML_CHUNK, qw), lambda b, c: (row_blk(b, c), 0)),
                  pl.BlockSpec((ML_CHUNK, qw), lambda b, c: (row_blk(b, c), 1)),
                  pl.BlockSpec((ML_CHUNK, vw), lambda b, c: (row_blk(b, c), 1)),
                  pl.BlockSpec((ML_CHUNK, vw), lambda b, c: (row_blk(b, c), 2)),
                  pl.BlockSpec((ML_CHUNK, LANES), lambda b, c: (row_blk(b, c), 0)),
                  pl.BlockSpec((1, vw), lambda b, c: (0, 0))],
        out_specs=pl.BlockSpec((ML_CHUNK, vw), lambda b, c: (row_blk(b, c), 0)),
        out_shape=jax.ShapeDtypeStruct((t, vw), BF16),
        scratch_shapes=[pltpu.VMEM((heads, dk, dv), F32),
                        pltpu.VMEM((heads, dk), F32),
                        pltpu.VMEM((heads, LANES), F32),
                        pltpu.VMEM((ML_CHUNK, vw), F32)],
        compiler_params=_params(("arbitrary", "arbitrary")),
        name="mlstm",
    )(proj, proj, proj, proj, gates, head_g.reshape(1, vw))


NORM_ROWS = 256
GATE_ROWS = 512
MM_TM, MM_TN = 1024, 1024
KS_TM, KS_TN, KS_TK = 1024, 1024, 4096


def _ffn(h, hg, ss, w_up_bf16, w_down, layer, next_norm_g, extra_casts):
    d_ff = w_up_bf16.shape[1]
    a, w_down_bf16, *extra = matmul(
        hg, w_up_bf16, n_out=d_ff, tm=MM_TM, tn=MM_TN, epilogue=_sqrelu, out_dtype=BF16,
        row_ss=ss, name="ffn_up",
        make_casts=lambda n: [SideCast(w_down, layer, d_ff, n)] + extra_casts(n))
    outs = matmul_ksplit_res(a, w_down_bf16, h, tm=KS_TM, tn=KS_TN, tk=KS_TK,
                             name="ffn_down", next_norm_g=next_norm_g)
    return outs, extra


def kernel(x, norm_mix, norm_ffn, gm_w_in, gm_ln_g, gm_ln_b, gm_w_s, gm_b_s, gm_w_out,
           ml_w_in, ml_b_gate, ml_head_g, ml_w_out, ffn_w_up, ffn_w_down, norm_final):
    batch, seq, d = x.shape
    h = x.reshape(batch * seq, d)
    mm = functools.partial(matmul, tm=MM_TM, tn=MM_TN)

    hn, gm_w_in_bf16 = rmsnorm(
        h, norm_mix[0], BF16, NORM_ROWS,
        make_casts=lambda n: [SideCast(gm_w_in, 0, d, n)])
    z, gm_w_out_bf16, w_up0_bf16 = mm(
        hn, gm_w_in_bf16, n_out=gm_w_in.shape[2], epilogue=_gelu, out_dtype=BF16,
        name="gmlp_in",
        make_casts=lambda n: [SideCast(gm_w_out, 0, d, n), SideCast(ffn_w_up, 0, d, n)])
    gated = gmlp_gate(z, gm_ln_g[0], gm_ln_b[0], gm_w_s[0], gm_b_s[0])
    h, hg, ss = mm(gated, gm_w_out_bf16, n_out=d, epilogue=_identity, out_dtype=F32,
                   res=h, next_norm_g=norm_ffn[0], name="gmlp_out")

    heads = ML_HEADS
    dv = ml_head_g.shape[2]
    dk = dv // 2
    n_main = 2 * heads * dk + 2 * heads * dv
    ml_w_in_t = jnp.swapaxes(ml_w_in, 1, 2)
    (h, hg, ss), (ml_w_in_t_bf16,) = _ffn(
        h, hg, ss, w_up0_bf16, ffn_w_down, 0, norm_mix[1],
        lambda n: [SideCast(ml_w_in_t, 0, n_main, n)])

    gates = mlstm_gates(hg, ss, ml_w_in_t[0, n_main:, :], ml_b_gate[0], GATE_ROWS)
    proj, ml_w_out_bf16, w_up1_bf16 = mm(
        hg, ml_w_in_t_bf16, n_out=n_main, epilogue=_identity, out_dtype=BF16,
        row_ss=ss, transposed_w=True, name="mlstm_in",
        make_casts=lambda n: [SideCast(ml_w_out, 0, d, n), SideCast(ffn_w_up, 1, d, n)])
    hs = mlstm(proj, gates, ml_head_g[0], batch=batch, heads=heads, dk=dk, dv=dv)
    h, hg, ss = mm(hs, ml_w_out_bf16, n_out=d, epilogue=_identity, out_dtype=F32,
                   res=h, next_norm_g=norm_ffn[1], name="mlstm_out")
    (h,), _ = _ffn(h, hg, ss, w_up1_bf16, ffn_w_down, 1, None, lambda n: [])

    (out,) = rmsnorm(h, norm_final, F32, NORM_ROWS)
    return out.reshape(batch, seq, d)
```

```python
import functools
import math

import jax
import jax.numpy as jnp
from jax import lax
from jax.experimental import pallas as pl
from jax.experimental.pallas import tpu as pltpu

F32 = jnp.float32
BF16 = jnp.bfloat16

EPS = 1e-6
GM_CHUNK = 128
GM_HEADS = 8
ML_HEADS = 8
ML_CHUNK = 128
GATE_CAP = 15.0
LANES = 128
BF16_SUBLANES = 16

VMEM_LIMIT_BYTES = 63 * 1024 * 1024

NT_DIMS = (((1,), (1,)), ((), ()))


def _params(semantics):
    return pltpu.CompilerParams(
        dimension_semantics=semantics, vmem_limit_bytes=VMEM_LIMIT_BYTES)


class SideCast:
    def __init__(self, src, layer, n_rows, n_steps):
        blocks = n_steps
        while n_rows % blocks or (n_rows // blocks) % BF16_SUBLANES:
            blocks -= 1
        self.srcs, self.layer = [src], layer
        self.n_blocks, self.block_rows = blocks, n_rows // blocks
        self.cols = src.shape[2]
        self.out_shapes = [jax.ShapeDtypeStruct((n_rows, self.cols), BF16)]

    def specs(self, step_of):
        last, layer = self.n_blocks - 1, self.layer
        blk = lambda *g: jnp.minimum(step_of(*g), last)
        return ([pl.BlockSpec((None, self.block_rows, self.cols),
                              lambda *g: (layer, blk(*g), 0))],
                [pl.BlockSpec((self.block_rows, self.cols), lambda *g: (blk(*g), 0))])

    @staticmethod
    def run(in_refs, out_refs):
        out_refs[0][...] = in_refs[0][...].astype(BF16)


def _add_jobs(jobs, step_of, in_specs, args, out_specs, out_shape):
    sig = []
    for job in jobs:
        i_specs, o_specs = job.specs(step_of)
        in_specs += i_specs
        args += job.srcs
        out_specs += o_specs
        out_shape += job.out_shapes
        sig.append((len(i_specs), len(o_specs), job.run))
    return tuple(sig)


def _run_jobs(sig, src_refs, dst_refs):
    src_refs, dst_refs = list(src_refs), list(dst_refs)
    for n_in, n_out, run in sig:
        run(src_refs[:n_in], dst_refs[:n_out])
        del src_refs[:n_in], dst_refs[:n_out]


def _n_job_refs(sig):
    return sum(s[0] for s in sig), sum(s[1] for s in sig)


def _rms_scale(x):
    return lax.rsqrt(jnp.mean(x * x, axis=-1, keepdims=True) + EPS)


def _rmsnorm_kernel(x_ref, g_ref, *refs, jobs):
    n_src, _ = _n_job_refs(jobs)
    x = x_ref[...]
    refs[n_src][...] = (x * _rms_scale(x) * g_ref[...]).astype(refs[n_src].dtype)
    _run_jobs(jobs, refs[:n_src], refs[n_src + 1:])


def rmsnorm(x, g, out_dtype, rows, make_jobs=None):
    t, d = x.shape
    n_steps = t // rows
    in_specs = [pl.BlockSpec((rows, d), lambda i: (i, 0)),
                pl.BlockSpec((1, d), lambda i: (0, 0))]
    args = [x, g.reshape(1, d)]
    out_specs = [pl.BlockSpec((rows, d), lambda i: (i, 0))]
    out_shape = [jax.ShapeDtypeStruct((t, d), out_dtype)]
    jobs = _add_jobs(make_jobs(n_steps) if make_jobs else [], lambda i: i,
                     in_specs, args, out_specs, out_shape)
    return pl.pallas_call(
        functools.partial(_rmsnorm_kernel, jobs=jobs),
        grid=(n_steps,),
        in_specs=in_specs,
        out_specs=out_specs,
        out_shape=out_shape,
        compiler_params=_params(("arbitrary",)),
        name="rmsnorm",
    )(*args)


def _row_scale(ss_ref, d):
    ss = ss_ref[...]
    part = sum(ss[:, c * LANES:(c + 1) * LANES] for c in range(ss.shape[1] // LANES))
    return lax.rsqrt(jnp.sum(part, axis=-1, keepdims=True) * (1.0 / d) + EPS)


def _lane_partial_sumsq(h):
    sq = h * h
    return sum(sq[:, c * LANES:(c + 1) * LANES] for c in range(h.shape[1] // LANES))


def _split3_bf16(x):
    hi = x.astype(BF16)
    r = x - hi.astype(F32)
    mid = r.astype(BF16)
    lo = (r - mid.astype(F32)).astype(BF16)
    return hi, mid, lo


def _gates_kernel(hg_ref, ss_ref, wg_ref, bg_ref, gate_ref, *, heads, d):
    pre =lax.dot_general(hg_ref[...], wg_ref[...], NT_DIMS, preferred_element_type=F32)
    pre = pre * _row_scale(ss_ref, d) + bg_ref[...]
    pre = GATE_CAP * jnp.tanh(pre / GATE_CAP)
    log_f = jnp.minimum(pre, 0.0) - jnp.log1p(jnp.exp(-jnp.abs(pre)))
    row = lax.broadcasted_iota(jnp.int32, (ML_CHUNK, ML_CHUNK), 0)
    col = lax.broadcasted_iota(jnp.int32, (ML_CHUNK, ML_CHUNK), 1)
    tril = (col <= row).astype(BF16)
    lane = lax.broadcasted_iota(jnp.int32, (ML_CHUNK, LANES), 1)
    for c in range(pre.shape[0] // ML_CHUNK):
        sl = slice(c * ML_CHUNK, (c + 1) * ML_CHUNK)
        cum = sum(jnp.dot(tril, p, preferred_element_type=F32)
                  for p in _split3_bf16(log_f[sl]))
        gate_ref[sl, :] = jnp.where(lane < heads, pre[sl], cum)


def mlstm_gates(hg, ss, w_gate_t, b_gate, rows):
    t, d = hg.shape
    n_gate = w_gate_t.shape[0]
    wg = jnp.zeros((LANES, d), BF16).at[:n_gate, :].set(w_gate_t.astype(BF16))
    bg = jnp.zeros((1, LANES), F32).at[0, :n_gate].set(b_gate)
    return pl.pallas_call(
        functools.partial(_gates_kernel, heads=n_gate // 2, d=d),
        grid=(t // rows,),
        in_specs=[pl.BlockSpec((rows, d), lambda i: (i, 0)),
                  pl.BlockSpec((rows, ss.shape[1]), lambda i: (i, 0)),
                  pl.BlockSpec((LANES, d), lambda i: (0, 0)),
                  pl.BlockSpec((1, LANES), lambda i: (0, 0))],
        out_specs=pl.BlockSpec((rows, LANES), lambda i: (i, 0)),
        out_shape=jax.ShapeDtypeStruct((t, LANES), F32),
        compiler_params=_params(("arbitrary",)),
        name="mlstm_gates",
    )(hg, ss, wg, bg)


def _gelu(x):
    return 0.5 * x * (1.0 + lax.erf(x * math.sqrt(0.5)))


def _sqrelu(x):
    r = jnp.maximum(x, 0.0)
    return r * r


def _identity(x):
    return x


def _emit_norm_inputs(h, g_ref, hg_ref, ss_ref):
    hg_ref[...] = (h * g_ref[...]).astype(BF16)
    ss_ref[...] = _lane_partial_sumsq(h)


def _mm_kernel(*refs, epilogue, has_res, has_scale, has_norm, jobs, transposed_w, d_in):
    n_src, n_dst = _n_job_refs(jobs)
    it = iter(refs)
    a_ref, w_ref = next(it), next(it)
    res_ref = next(it) if has_res else None
    ss_in_ref = next(it) if has_scale else None
    g_ref = next(it) if has_norm else None
    job_src = [next(it) for _ in range(n_src)]
    o_ref = next(it)
    hg_ref, ss_out_ref = (next(it), next(it)) if has_norm else (None, None)
    job_dst = [next(it) for _ in range(n_dst)]

    _run_jobs(jobs, job_src, job_dst)
    if transposed_w:
        acc = lax.dot_general(a_ref[...], w_ref[...], NT_DIMS, preferred_element_type=F32)
    else:
        acc = jnp.dot(a_ref[...], w_ref[...], preferred_element_type=F32)
    if has_scale:
        acc = acc * _row_scale(ss_in_ref, d_in)
    out = epilogue(acc)
    if has_res:
        out = res_ref[...] + out
    o_ref[...] = out.astype(o_ref.dtype)
    if has_norm:
        _emit_norm_inputs(out, g_ref, hg_ref, ss_out_ref)


def matmul(a, w, *, n_out, tm, tn, epilogue, out_dtype, name, res=None, row_ss=None,
           next_norm_g=None, transposed_w=False, make_jobs=None):
    m, k = a.shape
    assert n_out % tn == 0 and m % tm == 0
    ni, nj = m // tm, n_out // tn
    if transposed_w:
        assert w.shape[1] == k
        w_spec = pl.BlockSpec((tn, k), lambda j, i: (j, 0))
    else:
        assert w.shape[0] == k
        w_spec = pl.BlockSpec((k, tn), lambda j, i: (0, j))
    in_specs = [pl.BlockSpec((tm, k), lambda j, i: (i, 0)), w_spec]
    args = [a, w]
    tile = pl.BlockSpec((tm, tn), lambda j, i: (i, j))
    if res is not None:
        in_specs.append(tile)
        args.append(res)
    if row_ss is not None:
        in_specs.append(pl.BlockSpec((tm, row_ss.shape[1]), lambda j, i: (i, 0)))
        args.append(row_ss)
    out_specs = [tile]
    out_shape = [jax.ShapeDtypeStruct((m, n_out), out_dtype)]
    if next_norm_g is not None:
        in_specs.append(pl.BlockSpec((1, tn), lambda j, i: (0, j)))
        args.append(next_norm_g.reshape(1, n_out))
        out_specs += [tile, pl.BlockSpec((tm, LANES), lambda j, i: (i, j))]
        out_shape += [jax.ShapeDtypeStruct((m, n_out), BF16),
                      jax.ShapeDtypeStruct((m, nj * LANES), F32)]
    jobs = _add_jobs(make_jobs(ni * nj) if make_jobs else [], lambda j, i: j * ni + i,
                     in_specs, args, out_specs, out_shape)
    return pl.pallas_call(
        functools.partial(_mm_kernel, epilogue=epilogue, has_res=res is not None,
                          has_scale=row_ss is not None, has_norm=next_norm_g is not None,
                          jobs=jobs, transposed_w=transposed_w, d_in=k),
        grid=(nj, ni),
        in_specs=in_specs,
        out_specs=out_specs,
        out_shape=out_shape,
        compiler_params=_params(("arbitrary", "arbitrary")),
        name=name,
    )(*args)


def _mm_ksplit_kernel(*refs, has_norm):
    it = iter(refs)
    a_ref, w_ref, res_ref = next(it), next(it), next(it)
    g_ref = next(it) if has_norm else None
    o_ref = next(it)
    hg_ref, ss_out_ref = (next(it), next(it)) if has_norm else (None, None)

    @pl.when(pl.program_id(2) == 0)
    def _():
        o_ref[...] = res_ref[...]

    out = o_ref[...] + jnp.dot(a_ref[...], w_ref[...], preferred_element_type=F32)
    o_ref[...] = out
    if has_norm:
        _emit_norm_inputs(out, g_ref, hg_ref, ss_out_ref)


def matmul_ksplit_res(a, w, res, *, tm, tn, tk, name, next_norm_g=None):
    m, k = a.shape
    n = w.shape[1]
    assert w.shape[0] == k and m % tm == 0 and n % tn == 0 and k % tk == 0
    nj, ni, nk = n // tn, m // tm, k // tk
    tile = pl.BlockSpec((tm, tn), lambda j, i, kk: (i, j))
    in_specs = [pl.BlockSpec((tm, tk), lambda j, i, kk: (i, kk)),
                pl.BlockSpec((tk, tn), lambda j, i, kk: (kk, j)),
                tile]
    args = [a, w, res]
    out_specs = [tile]
    out_shape = [jax.ShapeDtypeStruct((m, n), F32)]
    if next_norm_g is not None:
        in_specs.append(pl.BlockSpec((1, tn), lambda j, i, kk: (0, j)))
        args.append(next_norm_g.reshape(1, n))
        out_specs += [tile, pl.BlockSpec((tm, LANES), lambda j, i, kk: (i, j))]
        out_shape += [jax.ShapeDtypeStruct((m, n), BF16),
                      jax.ShapeDtypeStruct((m, nj * LANES), F32)]
    return pl.pallas_call(
        functools.partial(_mm_ksplit_kernel, has_norm=next_norm_g is not None),
        grid=(nj, ni, nk),
        in_specs=in_specs,
        out_specs=out_specs,
        out_shape=out_shape,
        compiler_params=_params(("arbitrary", "arbitrary", "arbitrary")),
        name=name,
    )(*args)


def _gmlp_gate_kernel(u_ref, v_ref, lng_ref, lnb_ref, ws_ref, bs_ref, o_ref):
    v = v_ref[...].astype(F32)
    mu = jnp.mean(v, axis=-1, keepdims=True)
    vc = v - mu
    vn = vc * lax.rsqrt(jnp.mean(vc * vc, axis=-1, keepdims=True) + EPS)
    vn = (vn * lng_ref[...] + lnb_ref[...]).astype(BF16)
    row = lax.broadcasted_iota(jnp.int32, (GM_CHUNK, GM_CHUNK), 0)
    col = lax.broadcasted_iota(jnp.int32, (GM_CHUNK, GM_CHUNK), 1)
    causal = col <= row
    hd = v.shape[1] // GM_HEADS
    for h in range(GM_HEADS):
        sl = slice(h * hd, (h + 1) * hd)
        w = jnp.where(causal, ws_ref[h], 0.0).astype(BF16)
        for c in range(v.shape[0] // GM_CHUNK):
            rows = slice(c * GM_CHUNK, (c + 1) * GM_CHUNK)
            mixed = (jnp.dot(w, vn[rows, sl], preferred_element_type=F32)
                     + bs_ref[:, h:h + 1])
            o_ref[rows, sl] = (u_ref[rows, sl].astype(F32) * mixed).astype(o_ref.dtype)


def gmlp_gate(z, ln_g, ln_b, w_s, b_s, rows):
    t, two_w = z.shape
    width = two_w // 2
    assert rows % GM_CHUNK == 0 and t % rows == 0
    return pl.pallas_call(
        _gmlp_gate_kernel,
        grid=(t // rows,),
        in_specs=[pl.BlockSpec((rows, width), lambda i: (i, 0)),
                  pl.BlockSpec((rows, width), lambda i: (i, 1)),
                  pl.BlockSpec((1, width), lambda i: (0, 0)),
                  pl.BlockSpec((1, width), lambda i: (0, 0)),
                  pl.BlockSpec((GM_HEADS, GM_CHUNK, GM_CHUNK), lambda i: (0, 0, 0)),
                  pl.BlockSpec((GM_CHUNK, GM_HEADS), lambda i: (0, 0))],
        out_specs=pl.BlockSpec((rows, width), lambda i: (i, 0)),
        out_shape=jax.ShapeDtypeStruct((t, width), BF16),
        compiler_params=_params(("arbitrary",)),
        name="gmlp_gate",
    )(z, z, ln_g.reshape(1, width), ln_b.reshape(1, width), w_s, b_s.T)


def _mlstm_kernel(q_ref, k_ref, v_ref, o_ref, gate_ref, hg_ref, out_ref,
                  c_ref, n_ref, m_ref, hbuf_ref, *, heads):
    dk = q_ref.shape[1] // heads
    dv = v_ref.shape[1] // heads
    L = ML_CHUNK

    @pl.when(pl.program_id(1) == 0)
    def _():
        c_ref[...] = jnp.zeros_like(c_ref)
        n_ref[...] = jnp.zeros_like(n_ref)
        m_ref[...] = jnp.zeros_like(m_ref)

    gates = gate_ref[...]
    gates_t = gates.T
    row = lax.broadcasted_iota(jnp.int32, (L, L), 0)
    col = lax.broadcasted_iota(jnp.int32, (L, L), 1)
    causal = col <= row

    H = range(heads)
    qs = [q_ref[:, h * dk:(h + 1) * dk] * (dk ** -0.5) for h in H]
    ks = [k_ref[:, h * dk:(h + 1) * dk] for h in H]
    i_row = [gates_t[h:h + 1, :] for h in H]
    b_row = [gates_t[heads + h:heads + h + 1, :] for h in H]

    lane = lax.broadcasted_iota(jnp.int32, (L, LANES), 1)
    time = lax.broadcasted_iota(jnp.int32, (L, LANES), 0)
    i_all = jnp.where(lane < heads, gates, 0.0)
    b_all = jnp.where(lane < heads, pltpu.roll(gates, LANES - heads, axis=1), 0.0)
    m_old = m_ref[...]
    g_all = b_all[L - 1:L, :]
    pmax = i_all - b_all
    shift = 1
    while shift < L:
        prev = jnp.where(time >= shift, pltpu.roll(pmax, shift, axis=0), -jnp.inf)
        pmax = jnp.maximum(pmax, prev)
        shift *= 2
    inter_log_all = b_all + m_old
    m_t_all = jnp.maximum(inter_log_all, b_all + pmax)
    inter_w_all = jnp.exp(inter_log_all - m_t_all)
    floor_all = jnp.exp(-m_t_all)
    d_off_all = b_all - m_t_all
    a_all = g_all - b_all + i_all
    m_new_all = jnp.maximum(g_all + m_old, jnp.max(a_all, axis=0, keepdims=True))
    decay_all = jnp.exp(g_all + m_old - m_new_all)
    kscale_all = jnp.exp(a_all - m_new_all)
    m_ref[...] = m_new_all

    inter_w = [inter_w_all[:, h:h + 1] for h in H]
    floor = [floor_all[:, h:h + 1] for h in H]
    kscale = [kscale_all[:, h:h + 1] for h in H]
    decay = [decay_all[:, h:h + 1] for h in H]
    dmat = [jnp.exp(jnp.where(causal, d_off_all[:, h:h + 1] - b_row[h] + i_row[h], -jnp.inf))
            for h in H]

    s = [lax.dot_general(qs[h], ks[h], NT_DIMS, preferred_element_type=F32) * dmat[h]
         for h in H]
    s_sum = [jnp.sum(s[h], axis=-1, keepdims=True) for h in H]

    n_old = [n_ref[h:h + 1, :] for h in H]
    r_den = []
    for h in H:
        n_mxu = n_old[h].astype(BF16).astype(F32)
        qn = jnp.sum(qs[h].astype(F32) * n_mxu, axis=-1, keepdims=True)
        den = inter_w[h] * qn + s_sum[h]
        r_den.append(1.0 / jnp.maximum(jnp.abs(den), floor[h]))

    def mxu_stage(h):
        kw = ks[h].astype(F32) * kscale[h]
        kw_mxu = kw.astype(BF16)
        s_mxu = s[h].astype(BF16)
        v = v_ref[:, h * dv:(h + 1) * dv]
        num = (inter_w[h] * jnp.dot(qs[h], c_ref[h].astype(BF16), preferred_element_type=F32)
               + jnp.dot(s_mxu, v, preferred_element_type=F32))
        hbuf_ref[:, h * dv:(h + 1) * dv] = num * r_den[h]
        c_ref[h] = decay[h] * c_ref[h] + lax.dot_general(
            kw_mxu, v, (((0,), (0,)), ((), ())), preferred_element_type=F32)
        n_ref[h:h + 1, :] = decay[h] * n_old[h] + jnp.sum(kw, axis=0, keepdims=True)

    def norm_stage(h):
        h_out = hbuf_ref[:, h * dv:(h + 1) * dv]
        half_g = 0.5 * hg_ref[:, h * dv:(h + 1) * dv]
        hs = h_out * _rms_scale(h_out) * half_g
        og2 = jnp.tanh((0.5 * o_ref[:, h * dv:(h + 1) * dv]).astype(F32)) + 1.0
        out_ref[:, h * dv:(h + 1) * dv] = (hs * og2).astype(out_ref.dtype)

    for h in range(heads + 1):
        if h < heads:
            mxu_stage(h)
        if h > 0:
            norm_stage(h - 1)


def mlstm(proj, gates, head_g, *, batch, heads, dk, dv):
    t = proj.shape[0]
    nc = t // batch // ML_CHUNK
    qw, vw = heads * dk, heads * dv
    assert qw * 2 == vw
    row_blk = lambda b, c: b * nc + c
    return pl.pallas_call(
        functools.partial(_mlstm_kernel, heads=heads),
        grid=(batch, nc),
        in_specs=[pl.BlockSpec((ML_CHUNK, qw), lambda b, c: (row_blk(b, c), 0)),
                  pl.BlockSpec((ML_CHUNK, qw), lambda b, c: (row_blk(b, c), 1)),
                  pl.BlockSpec((ML_CHUNK, vw), lambda b, c: (row_blk(b, c), 1)),
                  pl.BlockSpec((ML_CHUNK, vw), lambda b, c: (row_blk(b, c), 2)),
                  pl.BlockSpec((ML_CHUNK, LANES), lambda b, c: (row_blk(b, c), 0)),
                  pl.BlockSpec((1, vw), lambda b, c: (0, 0))],
        out_specs=pl.BlockSpec((ML_CHUNK, vw), lambda b, c: (row_blk(b, c), 0)),
        out_shape=jax.ShapeDtypeStruct((t, vw), BF16),
        scratch_shapes=[pltpu.VMEM((heads, dk, dv), F32),
                        pltpu.VMEM((heads, dk), F32),
                        pltpu.VMEM((1, LANES), F32),
                        pltpu.VMEM((ML_CHUNK, vw), F32)],
        compiler_params=_params(("arbitrary", "arbitrary")),
        name="mlstm",
    )(proj, proj, proj, proj, gates, head_g.reshape(1, vw))


NORM_ROWS = 512
GATE_ROWS = 512
GMLP_ROWS = 256
MM_TM, MM_TN = 1024, 1024
KS_TM, KS_TN, KS_TK = 1024, 1024, 4096


def _ffn(h, hg, ss, w_up_bf16, w_down, layer, next_norm_g, extra_jobs):
    d_ff = w_up_bf16.shape[1]
    a, w_down_bf16, *extra = matmul(
        hg, w_up_bf16, n_out=d_ff, tm=MM_TM, tn=MM_TN, epilogue=_sqrelu, out_dtype=BF16,
        row_ss=ss, name="ffn_up",
        make_jobs=lambda n: [SideCast(w_down, layer, d_ff, n)] + extra_jobs(n))
    outs = matmul_ksplit_res(a, w_down_bf16, h, tm=KS_TM, tn=KS_TN, tk=KS_TK,
                             name="ffn_down", next_norm_g=next_norm_g)
    return outs, extra


def kernel(x, norm_mix, norm_ffn, gm_w_in, gm_ln_g, gm_ln_b, gm_w_s, gm_b_s, gm_w_out,
           ml_w_in, ml_b_gate, ml_head_g, ml_w_out, ffn_w_up, ffn_w_down, norm_final):
    batch, seq, d = x.shape
    h = x.reshape(batch * seq, d)
    mm = functools.partial(matmul, tm=MM_TM, tn=MM_TN)

    hn, gm_w_in_bf16 = rmsnorm(
        h, norm_mix[0], BF16, NORM_ROWS,
        make_jobs=lambda n: [SideCast(gm_w_in, 0, d, n)])
    z, gm_w_out_bf16, w_up0_bf16 = mm(
        hn, gm_w_in_bf16, n_out=gm_w_in.shape[2], epilogue=_gelu, out_dtype=BF16,
        name="gmlp_in",
        make_jobs=lambda n: [SideCast(gm_w_out, 0, d, n), SideCast(ffn_w_up, 0, d, n)])
    gated = gmlp_gate(z, gm_ln_g[0], gm_ln_b[0], gm_w_s[0], gm_b_s[0], GMLP_ROWS)
    h, hg, ss = mm(gated, gm_w_out_bf16, n_out=d, epilogue=_identity, out_dtype=F32,
                   res=h, next_norm_g=norm_ffn[0], name="gmlp_out")

    heads = ML_HEADS
    dv = ml_head_g.shape[2]
    dk = dv // 2
    n_main = 2 * heads * dk + 2 * heads * dv
    ml_w_in_t = jnp.swapaxes(ml_w_in, 1, 2)
    (h, hg, ss), (ml_w_in_t_bf16,) = _ffn(
        h, hg, ss, w_up0_bf16, ffn_w_down, 0, norm_mix[1],
        lambda n: [SideCast(ml_w_in_t, 0, n_main, n)])

    gates = mlstm_gates(hg, ss, ml_w_in_t[0, n_main:, :], ml_b_gate[0], GATE_ROWS)
    proj, ml_w_out_bf16, w_up1_bf16 = mm(
        hg, ml_w_in_t_bf16, n_out=n_main, epilogue=_identity, out_dtype=BF16,
        row_ss=ss, transposed_w=True, name="mlstm_in",
        make_jobs=lambda n: [SideCast(ml_w_out, 0, d, n), SideCast(ffn_w_up, 1, d, n)])
    hs = mlstm(proj, gates, ml_head_g[0], batch=batch, heads=heads, dk=dk, dv=dv)
    h, hg, ss = mm(hs, ml_w_out_bf16, n_out=d, epilogue=_identity, out_dtype=F32,
                   res=h, next_norm_g=norm_ffn[1], name="mlstm_out")
    (h,), _ = _ffn(h, hg, ss, w_up1_bf16, ffn_w_down, 1, None, lambda n: [])

    (out,) = rmsnorm(h, norm_final, F32, NORM_ROWS)
    return out.reshape(batch, seq, d)
```

```python
import functools
import math

import jax
import jax.numpy as jnp
from jax import lax
from jax.experimental import pallas as pl
from jax.experimental.pallas import tpu as pltpu

F32 = jnp.float32
BF16 = jnp.bfloat16

EPS = 1e-6
GM_CHUNK = 128
GM_HEADS = 8
ML_HEADS = 8
ML_CHUNK = 128
GATE_CAP = 15.0
LANES = 128
BF16_SUBLANES = 16

VMEM_LIMIT_BYTES = 63 * 1024 * 1024

NT_DIMS = (((1,), (1,)), ((), ()))


def _params(semantics):
    return pltpu.CompilerParams(
        dimension_semantics=semantics, vmem_limit_bytes=VMEM_LIMIT_BYTES)


class SideCast:
    def __init__(self, src, layer, n_rows, n_steps):
        blocks = n_steps
        while n_rows % blocks or (n_rows // blocks) % BF16_SUBLANES:
            blocks -= 1
        self.srcs, self.layer = [src], layer
        self.n_blocks, self.block_rows = blocks, n_rows // blocks
        self.cols = src.shape[2]
        self.out_shapes = [jax.ShapeDtypeStruct((n_rows, self.cols), BF16)]

    def specs(self, step_of):
        last, layer = self.n_blocks - 1, self.layer
        blk = lambda *g: jnp.minimum(step_of(*g), last)
        return ([pl.BlockSpec((None, self.block_rows, self.cols),
                              lambda *g: (layer, blk(*g), 0))],
                [pl.BlockSpec((self.block_rows, self.cols), lambda *g: (blk(*g), 0))])

    @staticmethod
    def run(in_refs, out_refs):
        out_refs[0][...] = in_refs[0][...].astype(BF16)


def _add_jobs(jobs, step_of, in_specs, args, out_specs, out_shape):
    sig = []
    for job in jobs:
        i_specs, o_specs = job.specs(step_of)
        in_specs += i_specs
        args += job.srcs
        out_specs += o_specs
        out_shape += job.out_shapes
        sig.append((len(i_specs), len(o_specs), job.run))
    return tuple(sig)


def _run_jobs(sig, src_refs, dst_refs):
    src_refs, dst_refs = list(src_refs), list(dst_refs)
    for n_in, n_out, run in sig:
        run(src_refs[:n_in], dst_refs[:n_out])
        del src_refs[:n_in], dst_refs[:n_out]


def _n_job_refs(sig):
    return sum(s[0] for s in sig), sum(s[1] for s in sig)


def _rms_scale(x):
    return lax.rsqrt(jnp.mean(x * x, axis=-1, keepdims=True) + EPS)


def _rmsnorm_kernel(x_ref, g_ref, *refs, jobs):
    n_src, _ = _n_job_refs(jobs)
    x = x_ref[...]
    refs[n_src][...] = (x * _rms_scale(x) * g_ref[...]).astype(refs[n_src].dtype)
    _run_jobs(jobs, refs[:n_src], refs[n_src + 1:])


def rmsnorm(x, g, out_dtype, rows, make_jobs=None):
    t, d = x.shape
    n_steps = t // rows
    in_specs = [pl.BlockSpec((rows, d), lambda i: (i, 0)),
                pl.BlockSpec((1, d), lambda i: (0, 0))]
    args = [x, g.reshape(1, d)]
    out_specs = [pl.BlockSpec((rows, d), lambda i: (i, 0))]
    out_shape = [jax.ShapeDtypeStruct((t, d), out_dtype)]
    jobs = _add_jobs(make_jobs(n_steps) if make_jobs else [], lambda i: i,
                     in_specs, args, out_specs, out_shape)
    return pl.pallas_call(
        functools.partial(_rmsnorm_kernel, jobs=jobs),
        grid=(n_steps,),
        in_specs=in_specs,
        out_specs=out_specs,
        out_shape=out_shape,
        compiler_params=_params(("arbitrary",)),
        name="rmsnorm",
    )(*args)


def _row_scale(ss_ref, d):
    ss = ss_ref[...]
    part = sum(ss[:, c * LANES:(c + 1) * LANES] for c in range(ss.shape[1] // LANES))
    return lax.rsqrt(jnp.sum(part, axis=-1, keepdims=True) * (1.0 / d) + EPS)


def _lane_partial_sumsq(h):
    sq = h * h
    return sum(sq[:, c * LANES:(c + 1) * LANES] for c in range(h.shape[1] // LANES))


def _split3_bf16(x):
    hi = x.astype(BF16)
    r = x - hi.astype(F32)
    mid = r.astype(BF16)
    lo = (r - mid.astype(F32)).astype(BF16)
    return hi, mid, lo


def _gates_kernel(hg_ref, ss_ref, wg_ref, bg_ref, gate_ref, *, heads, d):
    pre =lax.dot_general(hg_ref[...], wg_ref[...], NT_DIMS, preferred_element_type=F32)
    pre = pre * _row_scale(ss_ref, d) + bg_ref[...]
    pre = GATE_CAP * jnp.tanh(pre / GATE_CAP)
    log_f = jnp.minimum(pre, 0.0) - jnp.log1p(jnp.exp(-jnp.abs(pre)))
    row = lax.broadcasted_iota(jnp.int32, (ML_CHUNK, ML_CHUNK), 0)
    col = lax.broadcasted_iota(jnp.int32, (ML_CHUNK, ML_CHUNK), 1)
    tril = (col <= row).astype(BF16)
    lane = lax.broadcasted_iota(jnp.int32, (ML_CHUNK, LANES), 1)
    for c in range(pre.shape[0] // ML_CHUNK):
        sl = slice(c * ML_CHUNK, (c + 1) * ML_CHUNK)
        cum = sum(jnp.dot(tril, p, preferred_element_type=F32)
                  for p in _split3_bf16(log_f[sl]))
        gate_ref[sl, :] = jnp.where(lane < heads, pre[sl], cum)


def mlstm_gates(hg, ss, w_gate_t, b_gate, rows):
    t, d = hg.shape
    n_gate = w_gate_t.shape[0]
    wg = jnp.zeros((LANES, d), BF16).at[:n_gate, :].set(w_gate_t.astype(BF16))
    bg = jnp.zeros((1, LANES), F32).at[0, :n_gate].set(b_gate)
    return pl.pallas_call(
        functools.partial(_gates_kernel, heads=n_gate // 2, d=d),
        grid=(t // rows,),
        in_specs=[pl.BlockSpec((rows, d), lambda i: (i, 0)),
                  pl.BlockSpec((rows, ss.shape[1]), lambda i: (i, 0)),
                  pl.BlockSpec((LANES, d), lambda i: (0, 0)),
                  pl.BlockSpec((1, LANES), lambda i: (0, 0))],
        out_specs=pl.BlockSpec((rows, LANES), lambda i: (i, 0)),
        out_shape=jax.ShapeDtypeStruct((t, LANES), F32),
        compiler_params=_params(("arbitrary",)),
        name="mlstm_gates",
    )(hg, ss, wg, bg)


def _gelu(x):
    return 0.5 * x * (1.0 + lax.erf(x * math.sqrt(0.5)))


def _sqrelu(x):
    r = jnp.maximum(x, 0.0)
    return r * r


def _identity(x):
    return x


def _emit_norm_inputs(h, g_ref, hg_ref, ss_ref):
    hg_ref[...] = (h * g_ref[...]).astype(BF16)
    ss_ref[...] = _lane_partial_sumsq(h)


def _mm_kernel(*refs, epilogue, has_res, has_scale, has_norm, jobs, transposed_w, d_in):
    n_src, n_dst = _n_job_refs(jobs)
    it = iter(refs)
    a_ref, w_ref = next(it), next(it)
    res_ref = next(it) if has_res else None
    ss_in_ref = next(it) if has_scale else None
    g_ref = next(it) if has_norm else None
    job_src = [next(it) for _ in range(n_src)]
    o_ref = next(it)
    hg_ref, ss_out_ref = (next(it), next(it)) if has_norm else (None, None)
    job_dst = [next(it) for _ in range(n_dst)]

    _run_jobs(jobs, job_src, job_dst)
    if transposed_w:
        acc = lax.dot_general(a_ref[...], w_ref[...], NT_DIMS, preferred_element_type=F32)
    else:
        acc = jnp.dot(a_ref[...], w_ref[...], preferred_element_type=F32)
    if has_scale:
        acc = acc * _row_scale(ss_in_ref, d_in)
    out = epilogue(acc)
    if has_res:
        out = res_ref[...] + out
    o_ref[...] = out.astype(o_ref.dtype)
    if has_norm:
        _emit_norm_inputs(out, g_ref, hg_ref, ss_out_ref)


def matmul(a, w, *, n_out, tm, tn, epilogue, out_dtype, name, res=None, row_ss=None,
           next_norm_g=None, transposed_w=False, make_jobs=None):
    m, k = a.shape
    assert n_out % tn == 0 and m % tm == 0
    ni, nj = m // tm, n_out // tn
    if transposed_w:
        assert w.shape[1] == k
        w_spec = pl.BlockSpec((tn, k), lambda j, i: (j, 0))
    else:
        assert w.shape[0] == k
        w_spec = pl.BlockSpec((k, tn), lambda j, i: (0, j))
    in_specs = [pl.BlockSpec((tm, k), lambda j, i: (i, 0)), w_spec]
    args = [a, w]
    tile = pl.BlockSpec((tm, tn), lambda j, i: (i, j))
    if res is not None:
        in_specs.append(tile)
        args.append(res)
    if row_ss is not None:
        in_specs.append(pl.BlockSpec((tm, row_ss.shape[1]), lambda j, i: (i, 0)))
        args.append(row_ss)
    out_specs = [tile]
    out_shape = [jax.ShapeDtypeStruct((m, n_out), out_dtype)]
    if next_norm_g is not None:
        in_specs.append(pl.BlockSpec((1, tn), lambda j, i: (0, j)))
        args.append(next_norm_g.reshape(1, n_out))
        out_specs += [tile, pl.BlockSpec((tm, LANES), lambda j, i: (i, j))]
        out_shape += [jax.ShapeDtypeStruct((m, n_out), BF16),
                      jax.ShapeDtypeStruct((m, nj * LANES), F32)]
    jobs = _add_jobs(make_jobs(ni * nj) if make_jobs else [], lambda j, i: j * ni + i,
                     in_specs, args, out_specs, out_shape)
    return pl.pallas_call(
        functools.partial(_mm_kernel, epilogue=epilogue, has_res=res is not None,
                          has_scale=row_ss is not None, has_norm=next_norm_g is not None,
                          jobs=jobs, transposed_w=transposed_w, d_in=k),
        grid=(nj, ni),
        in_specs=in_specs,
        out_specs=out_specs,
        out_shape=out_shape,
        compiler_params=_params(("arbitrary", "arbitrary")),
        name=name,
    )(*args)


def _mm_ksplit_kernel(*refs, has_norm):
    it = iter(refs)
    a_ref, w_ref, res_ref = next(it), next(it), next(it)
    g_ref = next(it) if has_norm else None
    o_ref = next(it)
    hg_ref, ss_out_ref = (next(it), next(it)) if has_norm else (None, None)

    @pl.when(pl.program_id(2) == 0)
    def _():
        o_ref[...] = res_ref[...]

    out = o_ref[...] + jnp.dot(a_ref[...], w_ref[...], preferred_element_type=F32)
    o_ref[...] = out
    if has_norm:
        _emit_norm_inputs(out, g_ref, hg_ref, ss_out_ref)


def matmul_ksplit_res(a, w, res, *, tm, tn, tk, name, next_norm_g=None):
    m, k = a.shape
    n = w.shape[1]
    assert w.shape[0] == k and m % tm == 0 and n % tn == 0 and k % tk == 0
    nj, ni, nk = n // tn, m // tm, k // tk
    tile = pl.BlockSpec((tm, tn), lambda j, i, kk: (i, j))
    in_specs = [pl.BlockSpec((tm, tk), lambda j, i, kk: (i, kk)),
                pl.BlockSpec((tk, tn), lambda j, i, kk: (kk, j)),
                tile]
    args = [a, w, res]
    out_specs = [tile]
    out_shape = [jax.ShapeDtypeStruct((m, n), F32)]
    if next_norm_g is not None:
        in_specs.append(pl.BlockSpec((1, tn), lambda j, i, kk: (0, j)))
        args.append(next_norm_g.reshape(1, n))
        out_specs += [tile, pl.BlockSpec((tm, LANES), lambda j, i, kk: (i, j))]
        out_shape += [jax.ShapeDtypeStruct((m, n), BF16),
                      jax.ShapeDtypeStruct((m, nj * LANES), F32)]
    return pl.pallas_call(
        functools.partial(_mm_ksplit_kernel, has_norm=next_norm_g is not None),
        grid=(nj, ni, nk),
        in_specs=in_specs,
        out_specs=out_specs,
        out_shape=out_shape,
        compiler_params=_params(("arbitrary", "arbitrary", "arbitrary")),
        name=name,
    )(*args)


def _gmlp_gate_kernel(u_ref, v_ref, lng_ref, lnb_ref, ws_ref, bs_ref, o_ref):
    v = v_ref[...].astype(F32)
    mu = jnp.mean(v, axis=-1, keepdims=True)
    vc = v - mu
    vn = vc * lax.rsqrt(jnp.mean(vc * vc, axis=-1, keepdims=True) + EPS)
    vn = (vn * lng_ref[...] + lnb_ref[...]).astype(BF16)
    row = lax.broadcasted_iota(jnp.int32, (GM_CHUNK, GM_CHUNK), 0)
    col = lax.broadcasted_iota(jnp.int32, (GM_CHUNK, GM_CHUNK), 1)
    causal = col <= row
    hd = v.shape[1] // GM_HEADS
    for h in range(GM_HEADS):
        sl = slice(h * hd, (h + 1) * hd)
        w = jnp.where(causal, ws_ref[h], 0.0).astype(BF16)
        for c in range(v.shape[0] // GM_CHUNK):
            rows = slice(c * GM_CHUNK, (c + 1) * GM_CHUNK)
            mixed = (jnp.dot(w, vn[rows, sl], preferred_element_type=F32)
                     + bs_ref[:, h:h + 1])
            o_ref[rows, sl] = (u_ref[rows, sl].astype(F32) * mixed).astype(o_ref.dtype)


def gmlp_gate(z, ln_g, ln_b, w_s, b_s, rows):
    t, two_w = z.shape
    width = two_w // 2
    assert rows % GM_CHUNK == 0 and t % rows == 0
    return pl.pallas_call(
        _gmlp_gate_kernel,
        grid=(t // rows,),
        in_specs=[pl.BlockSpec((rows, width), lambda i: (i, 0)),
                  pl.BlockSpec((rows, width), lambda i: (i, 1)),
                  pl.BlockSpec((1, width), lambda i: (0, 0)),
                  pl.BlockSpec((1, width), lambda i: (0, 0)),
                  pl.BlockSpec((GM_HEADS, GM_CHUNK, GM_CHUNK), lambda i: (0, 0, 0)),
                  pl.BlockSpec((GM_CHUNK, GM_HEADS), lambda i: (0, 0))],
        out_specs=pl.BlockSpec((rows, width), lambda i: (i, 0)),
        out_shape=jax.ShapeDtypeStruct((t, width), BF16),
        compiler_params=_params(("arbitrary",)),
        name="gmlp_gate",
    )(z, z, ln_g.reshape(1, width), ln_b.reshape(1, width), w_s, b_s.T)


def _mlstm_kernel(q_ref, k_ref, v_ref, o_ref, gate_ref, hg_ref, out_ref,
                  c_ref, n_ref, m_ref, hbuf_ref, *, heads):
    dk = q_ref.shape[1] // heads
    dv = v_ref.shape[1] // heads
    L = ML_CHUNK

    @pl.when(pl.program_id(1) == 0)
    def _():
        c_ref[...] = jnp.zeros_like(c_ref)
        n_ref[...] = jnp.zeros_like(n_ref)
        m_ref[...] = jnp.zeros_like(m_ref)

    for first_row in range(0, q_ref.shape[0], L):
        _mlstm_chunk(slice(first_row, first_row + L), q_ref, k_ref, v_ref, o_ref, gate_ref,
                     hg_ref, out_ref, c_ref, n_ref, m_ref, hbuf_ref, heads=heads, dk=dk, dv=dv)


def _mlstm_chunk(rs, q_ref, k_ref, v_ref, o_ref, gate_ref, hg_ref, out_ref,
                 c_ref, n_ref, m_ref, hbuf_ref, *, heads, dk, dv):
    L = ML_CHUNK
    gates = gate_ref[rs, :]
    gates_t = gates.T
    row = lax.broadcasted_iota(jnp.int32, (L, L), 0)
    col = lax.broadcasted_iota(jnp.int32, (L, L), 1)
    causal = col <= row

    H = range(heads)
    qs = [q_ref[rs, h * dk:(h + 1) * dk] * (dk ** -0.5) for h in H]
    ks = [k_ref[rs, h * dk:(h + 1) * dk] for h in H]
    i_row = [gates_t[h:h + 1, :] for h in H]
    b_row = [gates_t[heads + h:heads + h + 1, :] for h in H]

    lane = lax.broadcasted_iota(jnp.int32, (L, LANES), 1)
    time = lax.broadcasted_iota(jnp.int32, (L, LANES), 0)
    i_all = jnp.where(lane < heads, gates, 0.0)
    b_all = jnp.where(lane < heads, pltpu.roll(gates, LANES - heads, axis=1), 0.0)
    m_old = m_ref[...]
    g_all = b_all[L - 1:L, :]
    pmax = i_all - b_all
    shift = 1
    while shift < L:
        prev = jnp.where(time >= shift, pltpu.roll(pmax, shift, axis=0), -jnp.inf)
        pmax = jnp.maximum(pmax, prev)
        shift *= 2
    inter_log_all = b_all + m_old
    m_t_all = jnp.maximum(inter_log_all, b_all + pmax)
    inter_w_all = jnp.exp(inter_log_all - m_t_all)
    floor_all = jnp.exp(-m_t_all)
    d_off_all = b_all - m_t_all
    a_all = g_all - b_all + i_all
    m_new_all = jnp.maximum(g_all + m_old, jnp.max(a_all, axis=0, keepdims=True))
    decay_all = jnp.exp(g_all + m_old - m_new_all)
    kscale_all = jnp.exp(a_all - m_new_all)
    m_ref[...] = m_new_all

    inter_w = [inter_w_all[:, h:h + 1] for h in H]
    floor = [floor_all[:, h:h + 1] for h in H]
    kscale = [kscale_all[:, h:h + 1] for h in H]
    decay = [decay_all[:, h:h + 1] for h in H]
    dmat = [jnp.exp(jnp.where(causal, d_off_all[:, h:h + 1] - b_row[h] + i_row[h], -jnp.inf))
            for h in H]

    s = [lax.dot_general(qs[h], ks[h], NT_DIMS, preferred_element_type=F32) * dmat[h]
         for h in H]
    s_sum = [jnp.sum(s[h], axis=-1, keepdims=True) for h in H]

    n_old = [n_ref[h:h + 1, :] for h in H]
    r_den = []
    for h in H:
        n_mxu = n_old[h].astype(BF16).astype(F32)
        qn = jnp.sum(qs[h].astype(F32) * n_mxu, axis=-1, keepdims=True)
        den = inter_w[h] * qn + s_sum[h]
        r_den.append(1.0 / jnp.maximum(jnp.abs(den), floor[h]))

    def mxu_stage(h):
        kw = ks[h].astype(F32) * kscale[h]
        kw_mxu = kw.astype(BF16)
        s_mxu = s[h].astype(BF16)
        v = v_ref[rs, h * dv:(h + 1) * dv]
        num = (inter_w[h] * jnp.dot(qs[h], c_ref[h].astype(BF16), preferred_element_type=F32)
               + jnp.dot(s_mxu, v, preferred_element_type=F32))
        hbuf_ref[:, h * dv:(h + 1) * dv] = num * r_den[h]
        c_ref[h] = decay[h] * c_ref[h] + lax.dot_general(
            kw_mxu, v, (((0,), (0,)), ((), ())), preferred_element_type=F32)
        n_ref[h:h + 1, :] = decay[h] * n_old[h] + jnp.sum(kw, axis=0, keepdims=True)

    def norm_stage(h):
        h_out = hbuf_ref[:, h * dv:(h + 1) * dv]
        half_g = 0.5 * hg_ref[:, h * dv:(h + 1) * dv]
        hs = h_out * _rms_scale(h_out) * half_g
        og2 = jnp.tanh((0.5 * o_ref[rs, h * dv:(h + 1) * dv]).astype(F32)) + 1.0
        out_ref[rs, h * dv:(h + 1) * dv] = (hs * og2).astype(out_ref.dtype)

    for h in range(heads + 1):
        if h < heads:
            mxu_stage(h)
        if h > 0:
            norm_stage(h - 1)


def mlstm(proj, gates, head_g, *, batch, heads, dk, dv, rows):
    t = proj.shape[0]
    seq = t // batch
    assert rows % ML_CHUNK == 0 and seq % rows == 0
    ns = seq // rows
    qw, vw = heads * dk, heads * dv
    assert qw * 2 == vw
    row_blk = lambda b, c: b * ns + c
    return pl.pallas_call(
        functools.partial(_mlstm_kernel, heads=heads),
        grid=(batch, ns),
        in_specs=[pl.BlockSpec((rows, qw), lambda b, c: (row_blk(b, c), 0)),
                  pl.BlockSpec((rows, qw), lambda b, c: (row_blk(b, c), 1)),
                  pl.BlockSpec((rows, vw), lambda b, c: (row_blk(b, c), 1)),
                  pl.BlockSpec((rows, vw), lambda b, c: (row_blk(b, c), 2)),
                  pl.BlockSpec((rows, LANES), lambda b, c: (row_blk(b, c), 0)),
                  pl.BlockSpec((1, vw), lambda b, c: (0, 0))],
        out_specs=pl.BlockSpec((rows, vw), lambda b, c: (row_blk(b, c), 0)),
        out_shape=jax.ShapeDtypeStruct((t, vw), BF16),
        scratch_shapes=[pltpu.VMEM((heads, dk, dv), F32),
                        pltpu.VMEM((heads, dk), F32),
                        pltpu.VMEM((1, LANES), F32),
                        pltpu.VMEM((ML_CHUNK, vw), F32)],
        compiler_params=_params(("arbitrary", "arbitrary")),
        name="mlstm",
    )(proj, proj, proj, proj, gates, head_g.reshape(1, vw))


NORM_ROWS = 512
GATE_ROWS = 512
GMLP_ROWS = 512
MLSTM_ROWS = 512
MM_TM, MM_TN = 1024, 1024
KS_TM, KS_TN, KS_TK = 1024, 1024, 4096


def _ffn(h, hg, ss, w_up_bf16, w_down, layer, next_norm_g, extra_jobs):
    d_ff = w_up_bf16.shape[1]
    a, w_down_bf16, *extra = matmul(
        hg, w_up_bf16, n_out=d_ff, tm=MM_TM, tn=MM_TN, epilogue=_sqrelu, out_dtype=BF16,
        row_ss=ss, name="ffn_up",
        make_jobs=lambda n: [SideCast(w_down, layer, d_ff, n)] + extra_jobs(n))
    outs = matmul_ksplit_res(a, w_down_bf16, h, tm=KS_TM, tn=KS_TN, tk=KS_TK,
                             name="ffn_down", next_norm_g=next_norm_g)
    return outs, extra


def kernel(x, norm_mix, norm_ffn, gm_w_in, gm_ln_g, gm_ln_b, gm_w_s, gm_b_s, gm_w_out,
           ml_w_in, ml_b_gate, ml_head_g, ml_w_out, ffn_w_up, ffn_w_down, norm_final):
    batch, seq, d = x.shape
    h = x.reshape(batch * seq, d)
    mm = functools.partial(matmul, tm=MM_TM, tn=MM_TN)

    hn, gm_w_in_bf16 = rmsnorm(
        h, norm_mix[0], BF16, NORM_ROWS,
        make_jobs=lambda n: [SideCast(gm_w_in, 0, d, n)])
    z, gm_w_out_bf16, w_up0_bf16 = mm(
        hn, gm_w_in_bf16, n_out=gm_w_in.shape[2], epilogue=_gelu, out_dtype=BF16,
        name="gmlp_in",
        make_jobs=lambda n: [SideCast(gm_w_out, 0, d, n), SideCast(ffn_w_up, 0, d, n)])
    gated = gmlp_gate(z, gm_ln_g[0], gm_ln_b[0], gm_w_s[0], gm_b_s[0], GMLP_ROWS)
    h, hg, ss = mm(gated, gm_w_out_bf16, n_out=d, epilogue=_identity, out_dtype=F32,
                   res=h, next_norm_g=norm_ffn[0], name="gmlp_out")

    heads = ML_HEADS
    dv = ml_head_g.shape[2]
    dk = dv // 2
    n_main = 2 * heads * dk + 2 * heads * dv
    ml_w_in_t = jnp.swapaxes(ml_w_in, 1, 2)
    (h, hg, ss), (ml_w_in_t_bf16,) = _ffn(
        h, hg, ss, w_up0_bf16, ffn_w_down, 0, norm_mix[1],
        lambda n: [SideCast(ml_w_in_t, 0, n_main, n)])

    gates = mlstm_gates(hg, ss, ml_w_in_t[0, n_main:, :], ml_b_gate[0], GATE_ROWS)
    proj, ml_w_out_bf16, w_up1_bf16 = mm(
        hg, ml_w_in_t_bf16, n_out=n_main, epilogue=_identity, out_dtype=BF16,
        row_ss=ss, transposed_w=True, name="mlstm_in",
        make_jobs=lambda n: [SideCast(ml_w_out, 0, d, n), SideCast(ffn_w_up, 1, d, n)])
    hs = mlstm(proj, gates, ml_head_g[0], batch=batch, heads=heads, dk=dk, dv=dv,
               rows=MLSTM_ROWS)
    h, hg, ss = mm(hs, ml_w_out_bf16, n_out=d, epilogue=_identity, out_dtype=F32,
                   res=h, next_norm_g=norm_ffn[1], name="mlstm_out")
    (h,), _ = _ffn(h, hg, ss, w_up1_bf16, ffn_w_down, 1, None, lambda n: [])

    (out,) = rmsnorm(h, norm_final, F32, NORM_ROWS)
    return out.reshape(batch, seq, d)
```

```python
import functools
import math

import jax
import jax.numpy as jnp
from jax import lax
from jax.experimental import pallas as pl
from jax.experimental.pallas import tpu as pltpu

F32 = jnp.float32
BF16 = jnp.bfloat16

EPS = 1e-6
GM_CHUNK = 128
GM_HEADS = 8
ML_HEADS = 8
ML_CHUNK = 128
GATE_CAP = 15.0
LANES = 128
BF16_SUBLANES = 16

VMEM_LIMIT_BYTES = 63 * 1024 * 1024

NT_DIMS = (((1,), (1,)), ((), ()))


def _params(semantics):
    return pltpu.CompilerParams(
        dimension_semantics=semantics, vmem_limit_bytes=VMEM_LIMIT_BYTES)


class SideCast:
    def __init__(self, src, layer, n_rows, n_steps):
        blocks = n_steps
        while n_rows % blocks or (n_rows // blocks) % BF16_SUBLANES:
            blocks -= 1
        self.srcs, self.layer = [src], layer
        self.n_blocks, self.block_rows = blocks, n_rows // blocks
        self.cols = src.shape[2]
        self.out_shapes = [jax.ShapeDtypeStruct((n_rows, self.cols), BF16)]

    def specs(self, step_of):
        last, layer = self.n_blocks - 1, self.layer
        blk = lambda *g: jnp.minimum(step_of(*g), last)
        return ([pl.BlockSpec((None, self.block_rows, self.cols),
                              lambda *g: (layer, blk(*g), 0))],
                [pl.BlockSpec((self.block_rows, self.cols), lambda *g: (blk(*g), 0))])

    @staticmethod
    def run(in_refs, out_refs):
        out_refs[0][...] = in_refs[0][...].astype(BF16)


def _add_jobs(jobs, step_of, in_specs, args, out_specs, out_shape):
    sig = []
    for job in jobs:
        i_specs, o_specs = job.specs(step_of)
        in_specs += i_specs
        args += job.srcs
        out_specs += o_specs
        out_shape += job.out_shapes
        sig.append((len(i_specs), len(o_specs), job.run))
    return tuple(sig)


def _run_jobs(sig, src_refs, dst_refs):
    src_refs, dst_refs = list(src_refs), list(dst_refs)
    for n_in, n_out, run in sig:
        run(src_refs[:n_in], dst_refs[:n_out])
        del src_refs[:n_in], dst_refs[:n_out]


def _n_job_refs(sig):
    return sum(s[0] for s in sig), sum(s[1] for s in sig)


def _rms_scale(x):
    return lax.rsqrt(jnp.mean(x * x, axis=-1, keepdims=True) + EPS)


def _rmsnorm_kernel(x_ref, g_ref, *refs, jobs):
    n_src, _ = _n_job_refs(jobs)
    x = x_ref[...]
    refs[n_src][...] = (x * _rms_scale(x) * g_ref[...]).astype(refs[n_src].dtype)
    _run_jobs(jobs, refs[:n_src], refs[n_src + 1:])


def rmsnorm(x, g, out_dtype, rows, make_jobs=None):
    t, d = x.shape
    n_steps = t // rows
    in_specs = [pl.BlockSpec((rows, d), lambda i: (i, 0)),
                pl.BlockSpec((1, d), lambda i: (0, 0))]
    args = [x, g.reshape(1, d)]
    out_specs = [pl.BlockSpec((rows, d), lambda i: (i, 0))]
    out_shape = [jax.ShapeDtypeStruct((t, d), out_dtype)]
    jobs = _add_jobs(make_jobs(n_steps) if make_jobs else [], lambda i: i,
                     in_specs, args, out_specs, out_shape)
    return pl.pallas_call(
        functools.partial(_rmsnorm_kernel, jobs=jobs),
        grid=(n_steps,),
        in_specs=in_specs,
        out_specs=out_specs,
        out_shape=out_shape,
        compiler_params=_params(("arbitrary",)),
        name="rmsnorm",
    )(*args)


def _row_scale(ss_ref, d):
    ss = ss_ref[...]
    part = sum(ss[:, c * LANES:(c + 1) * LANES] for c in range(ss.shape[1] // LANES))
    return lax.rsqrt(jnp.sum(part, axis=-1, keepdims=True) * (1.0 / d) + EPS)


def _lane_partial_sumsq(h):
    sq = h * h
    return sum(sq[:, c * LANES:(c + 1) * LANES] for c in range(h.shape[1] // LANES))


def _split3_bf16(x):
    hi = x.astype(BF16)
    r = x - hi.astype(F32)
    mid = r.astype(BF16)
    lo = (r - mid.astype(F32)).astype(BF16)
    return hi, mid, lo


def _gates_kernel(hg_ref, ss_ref, wg_ref, bg_ref, gate_ref, *, heads, d):
    pre =lax.dot_general(hg_ref[...], wg_ref[...], NT_DIMS, preferred_element_type=F32)
    pre = pre * _row_scale(ss_ref, d) + bg_ref[...]
    pre = GATE_CAP * jnp.tanh(pre / GATE_CAP)
    log_f = jnp.minimum(pre, 0.0) - jnp.log1p(jnp.exp(-jnp.abs(pre)))
    row = lax.broadcasted_iota(jnp.int32, (ML_CHUNK, ML_CHUNK), 0)
    col = lax.broadcasted_iota(jnp.int32, (ML_CHUNK, ML_CHUNK), 1)
    tril = (col <= row).astype(BF16)
    lane = lax.broadcasted_iota(jnp.int32, (ML_CHUNK, LANES), 1)
    for c in range(pre.shape[0] // ML_CHUNK):
        sl = slice(c * ML_CHUNK, (c + 1) * ML_CHUNK)
        cum = sum(jnp.dot(tril, p, preferred_element_type=F32)
                  for p in _split3_bf16(log_f[sl]))
        gate_ref[sl, :] = jnp.where(lane < heads, pre[sl], cum)


def mlstm_gates(hg, ss, w_gate_t, b_gate, rows):
    t, d = hg.shape
    n_gate = w_gate_t.shape[0]
    wg = jnp.zeros((LANES, d), BF16).at[:n_gate, :].set(w_gate_t.astype(BF16))
    bg = jnp.zeros((1, LANES), F32).at[0, :n_gate].set(b_gate)
    return pl.pallas_call(
        functools.partial(_gates_kernel, heads=n_gate // 2, d=d),
        grid=(t // rows,),
        in_specs=[pl.BlockSpec((rows, d), lambda i: (i, 0)),
                  pl.BlockSpec((rows, ss.shape[1]), lambda i: (i, 0)),
                  pl.BlockSpec((LANES, d), lambda i: (0, 0)),
                  pl.BlockSpec((1, LANES), lambda i: (0, 0))],
        out_specs=pl.BlockSpec((rows, LANES), lambda i: (i, 0)),
        out_shape=jax.ShapeDtypeStruct((t, LANES), F32),
        compiler_params=_params(("arbitrary",)),
        name="mlstm_gates",
    )(hg, ss, wg, bg)


def _gelu(x):
    return 0.5 * x * (1.0 + lax.erf(x * math.sqrt(0.5)))


def _sqrelu(x):
    r = jnp.maximum(x, 0.0)
    return r * r


def _identity(x):
    return x


def _emit_norm_inputs(h, g_ref, hg_ref, ss_ref):
    hg_ref[...] = (h * g_ref[...]).astype(BF16)
    ss_ref[...] = _lane_partial_sumsq(h)


def _mm_kernel(*refs, epilogue, has_res, has_scale, has_norm, jobs, transposed_w, d_in):
    n_src, n_dst = _n_job_refs(jobs)
    it = iter(refs)
    a_ref, w_ref = next(it), next(it)
    res_ref = next(it) if has_res else None
    ss_in_ref = next(it) if has_scale else None
    g_ref = next(it) if has_norm else None
    job_src = [next(it) for _ in range(n_src)]
    o_ref = next(it)
    hg_ref, ss_out_ref = (next(it), next(it)) if has_norm else (None, None)
    job_dst = [next(it) for _ in range(n_dst)]

    _run_jobs(jobs, job_src, job_dst)
    if transposed_w:
        acc = lax.dot_general(a_ref[...], w_ref[...], NT_DIMS, preferred_element_type=F32)
    else:
        acc = jnp.dot(a_ref[...], w_ref[...], preferred_element_type=F32)
    if has_scale:
        acc = acc * _row_scale(ss_in_ref, d_in)
    out = epilogue(acc)
    if has_res:
        out = res_ref[...] + out
    o_ref[...] = out.astype(o_ref.dtype)
    if has_norm:
        _emit_norm_inputs(out, g_ref, hg_ref, ss_out_ref)


def matmul(a, w, *, n_out, tm, tn, epilogue, out_dtype, name, res=None, row_ss=None,
           next_norm_g=None, transposed_w=False, make_jobs=None):
    m, k = a.shape
    assert n_out % tn == 0 and m % tm == 0
    ni, nj = m // tm, n_out // tn
    if transposed_w:
        assert w.shape[1] == k
        w_spec = pl.BlockSpec((tn, k), lambda j, i: (j, 0))
    else:
        assert w.shape[0] == k
        w_spec = pl.BlockSpec((k, tn), lambda j, i: (0, j))
    in_specs = [pl.BlockSpec((tm, k), lambda j, i: (i, 0)), w_spec]
    args = [a, w]
    tile = pl.BlockSpec((tm, tn), lambda j, i: (i, j))
    if res is not None:
        in_specs.append(tile)
        args.append(res)
    if row_ss is not None:
        in_specs.append(pl.BlockSpec((tm, row_ss.shape[1]), lambda j, i: (i, 0)))
        args.append(row_ss)
    out_specs = [tile]
    out_shape = [jax.ShapeDtypeStruct((m, n_out), out_dtype)]
    if next_norm_g is not None:
        in_specs.append(pl.BlockSpec((1, tn), lambda j, i: (0, j)))
        args.append(next_norm_g.reshape(1, n_out))
        out_specs += [tile, pl.BlockSpec((tm, LANES), lambda j, i: (i, j))]
        out_shape += [jax.ShapeDtypeStruct((m, n_out), BF16),
                      jax.ShapeDtypeStruct((m, nj * LANES), F32)]
    jobs = _add_jobs(make_jobs(ni * nj) if make_jobs else [], lambda j, i: j * ni + i,
                     in_specs, args, out_specs, out_shape)
    return pl.pallas_call(
        functools.partial(_mm_kernel, epilogue=epilogue, has_res=res is not None,
                          has_scale=row_ss is not None, has_norm=next_norm_g is not None,
                          jobs=jobs, transposed_w=transposed_w, d_in=k),
        grid=(nj, ni),
        in_specs=in_specs,
        out_specs=out_specs,
        out_shape=out_shape,
        compiler_params=_params(("arbitrary", "arbitrary")),
        name=name,
    )(*args)


def _mm_ksplit_kernel(*refs, has_norm):
    it = iter(refs)
    a_ref, w_ref, res_ref = next(it), next(it), next(it)
    g_ref = next(it) if has_norm else None
    o_ref = next(it)
    hg_ref, ss_out_ref = (next(it), next(it)) if has_norm else (None, None)

    @pl.when(pl.program_id(2) == 0)
    def _():
        o_ref[...] = res_ref[...]

    out = o_ref[...] + jnp.dot(a_ref[...], w_ref[...], preferred_element_type=F32)
    o_ref[...] = out
    if has_norm:
        _emit_norm_inputs(out, g_ref, hg_ref, ss_out_ref)


def matmul_ksplit_res(a, w, res, *, tm, tn, tk, name, next_norm_g=None):
    m, k = a.shape
    n = w.shape[1]
    assert w.shape[0] == k and m % tm == 0 and n % tn == 0 and k % tk == 0
    nj, ni, nk = n // tn, m // tm, k // tk
    tile = pl.BlockSpec((tm, tn), lambda j, i, kk: (i, j))
    in_specs = [pl.BlockSpec((tm, tk), lambda j, i, kk: (i, kk)),
                pl.BlockSpec((tk, tn), lambda j, i, kk: (kk, j)),
                tile]
    args = [a, w, res]
    out_specs = [tile]
    out_shape = [jax.ShapeDtypeStruct((m, n), F32)]
    if next_norm_g is not None:
        in_specs.append(pl.BlockSpec((1, tn), lambda j, i, kk: (0, j)))
        args.append(next_norm_g.reshape(1, n))
        out_specs += [tile, pl.BlockSpec((tm, LANES), lambda j, i, kk: (i, j))]
        out_shape += [jax.ShapeDtypeStruct((m, n), BF16),
                      jax.ShapeDtypeStruct((m, nj * LANES), F32)]
    return pl.pallas_call(
        functools.partial(_mm_ksplit_kernel, has_norm=next_norm_g is not None),
        grid=(nj, ni, nk),
        in_specs=in_specs,
        out_specs=out_specs,
        out_shape=out_shape,
        compiler_params=_params(("arbitrary", "arbitrary", "arbitrary")),
        name=name,
    )(*args)


def _gmlp_gate_kernel(u_ref, v_ref, lng_ref, lnb_ref, ws_ref, bs_ref, o_ref):
    v = v_ref[...].astype(F32)
    mu = jnp.mean(v, axis=-1, keepdims=True)
    vc = v - mu
    vn = vc * lax.rsqrt(jnp.mean(vc * vc, axis=-1, keepdims=True) + EPS)
    vn = (vn * lng_ref[...] + lnb_ref[...]).astype(BF16)
    row = lax.broadcasted_iota(jnp.int32, (GM_CHUNK, GM_CHUNK), 0)
    col = lax.broadcasted_iota(jnp.int32, (GM_CHUNK, GM_CHUNK), 1)
    causal = col <= row
    hd = v.shape[1] // GM_HEADS
    for h in range(GM_HEADS):
        sl = slice(h * hd, (h + 1) * hd)
        w = jnp.where(causal, ws_ref[h], 0.0).astype(BF16)
        for c in range(v.shape[0] // GM_CHUNK):
            rows = slice(c * GM_CHUNK, (c + 1) * GM_CHUNK)
            mixed = (jnp.dot(w, vn[rows, sl], preferred_element_type=F32)
                     + bs_ref[:, h:h + 1])
            o_ref[rows, sl] = (u_ref[rows, sl].astype(F32) * mixed).astype(o_ref.dtype)


def gmlp_gate(z, ln_g, ln_b, w_s, b_s, rows):
    t, two_w = z.shape
    width = two_w // 2
    assert rows % GM_CHUNK == 0 and t % rows == 0
    return pl.pallas_call(
        _gmlp_gate_kernel,
        grid=(t // rows,),
        in_specs=[pl.BlockSpec((rows, width), lambda i: (i, 0)),
                  pl.BlockSpec((rows, width), lambda i: (i, 1)),
                  pl.BlockSpec((1, width), lambda i: (0, 0)),
                  pl.BlockSpec((1, width), lambda i: (0, 0)),
                  pl.BlockSpec((GM_HEADS, GM_CHUNK, GM_CHUNK), lambda i: (0, 0, 0)),
                  pl.BlockSpec((GM_CHUNK, GM_HEADS), lambda i: (0, 0))],
        out_specs=pl.BlockSpec((rows, width), lambda i: (i, 0)),
        out_shape=jax.ShapeDtypeStruct((t, width), BF16),
        compiler_params=_params(("arbitrary",)),
        name="gmlp_gate",
    )(z, z, ln_g.reshape(1, width), ln_b.reshape(1, width), w_s, b_s.T)


def _mlstm_kernel(q_ref, k_ref, v_ref, o_ref, gate_ref, hg_ref, out_ref,
                  c_ref, n_ref, m_ref, hbuf_ref, *, heads):
    dk = q_ref.shape[1] // heads
    dv = v_ref.shape[1] // heads
    L = ML_CHUNK

    @pl.when(pl.program_id(1) == 0)
    def _():
        c_ref[...] = jnp.zeros_like(c_ref)
        n_ref[...] = jnp.zeros_like(n_ref)
        m_ref[...] = jnp.zeros_like(m_ref)

    for first_row in range(0, q_ref.shape[0], L):
        _mlstm_chunk(slice(first_row, first_row + L), q_ref, k_ref, v_ref, o_ref, gate_ref,
                     hg_ref, out_ref, c_ref, n_ref, m_ref, hbuf_ref, heads=heads, dk=dk, dv=dv)


def _mlstm_chunk(rs, q_ref, k_ref, v_ref, o_ref, gate_ref, hg_ref, out_ref,
                 c_ref, n_ref, m_ref, hbuf_ref, *, heads, dk, dv):
    L = ML_CHUNK
    gates = gate_ref[rs, :]
    gates_t = gates.T
    row = lax.broadcasted_iota(jnp.int32, (L, L), 0)
    col = lax.broadcasted_iota(jnp.int32, (L, L), 1)
    causal = col <= row

    H = range(heads)
    qs = [q_ref[rs, h * dk:(h + 1) * dk] * (dk ** -0.5) for h in H]
    ks = [k_ref[rs, h * dk:(h + 1) * dk] for h in H]
    i_row = [gates_t[h:h + 1, :] for h in H]
    b_row = [gates_t[heads + h:heads + h + 1, :] for h in H]

    lane = lax.broadcasted_iota(jnp.int32, (L, LANES), 1)
    time = lax.broadcasted_iota(jnp.int32, (L, LANES), 0)
    i_all = jnp.where(lane < heads, gates, 0.0)
    b_all = jnp.where(lane < heads, pltpu.roll(gates, LANES - heads, axis=1), 0.0)
    m_old = m_ref[...]
    g_all = b_all[L - 1:L, :]
    pmax = i_all - b_all
    shift = 1
    while shift < L:
        prev = jnp.where(time >= shift, pltpu.roll(pmax, shift, axis=0), -jnp.inf)
        pmax = jnp.maximum(pmax, prev)
        shift *= 2
    inter_log_all = b_all + m_old
    m_t_all = jnp.maximum(inter_log_all, b_all + pmax)
    inter_w_all = jnp.exp(inter_log_all - m_t_all)
    floor_all = jnp.exp(-m_t_all)
    d_off_all = b_all - m_t_all
    a_all = g_all - b_all + i_all
    m_new_all = jnp.maximum(g_all + m_old, jnp.max(a_all, axis=0, keepdims=True))
    decay_all = jnp.exp(g_all + m_old - m_new_all)
    kscale_all = jnp.exp(a_all - m_new_all)
    m_ref[...] = m_new_all

    col_of = lambda tile, h: tile[:, h:h + 1]
    dmat = [jnp.exp(jnp.where(causal, col_of(d_off_all, h) - b_row[h] + i_row[h], -jnp.inf))
            for h in H]

    s = [lax.dot_general(qs[h], ks[h], NT_DIMS, preferred_element_type=F32) * dmat[h]
         for h in H]
    s_sum = [jnp.sum(s[h], axis=-1, keepdims=True) for h in H]

    r_den = []
    for h in H:
        n_mxu = n_ref[h:h + 1, :].astype(BF16).astype(F32)
        qn = jnp.sum(qs[h].astype(F32) * n_mxu, axis=-1, keepdims=True)
        den = col_of(inter_w_all, h) * qn + s_sum[h]
        r_den.append(1.0 / jnp.maximum(jnp.abs(den), col_of(floor_all, h)))

    def mxu_stage(h):
        decay = col_of(decay_all, h)
        kw = ks[h].astype(F32) * col_of(kscale_all, h)
        kw_mxu = kw.astype(BF16)
        s_mxu = s[h].astype(BF16)
        v = v_ref[rs, h * dv:(h + 1) * dv]
        num = (col_of(inter_w_all, h)
               * jnp.dot(qs[h], c_ref[h].astype(BF16), preferred_element_type=F32)
               + jnp.dot(s_mxu, v, preferred_element_type=F32))
        hbuf_ref[:, h * dv:(h + 1) * dv] = num * r_den[h]
        c_ref[h] = decay * c_ref[h] + lax.dot_general(
            kw_mxu, v, (((0,), (0,)), ((), ())), preferred_element_type=F32)
        n_ref[h:h + 1, :] = decay * n_ref[h:h + 1, :] + jnp.sum(kw, axis=0, keepdims=True)

    def norm_stage(h):
        h_out = hbuf_ref[:, h * dv:(h + 1) * dv]
        half_g = 0.5 * hg_ref[:, h * dv:(h + 1) * dv]
        hs = h_out * _rms_scale(h_out) * half_g
        og2 = jnp.tanh((0.5 * o_ref[rs, h * dv:(h + 1) * dv]).astype(F32)) + 1.0
        out_ref[rs, h * dv:(h + 1) * dv] = (hs * og2).astype(out_ref.dtype)

    for h in range(heads + 1):
        if h < heads:
            mxu_stage(h)
        if h > 0:
            norm_stage(h - 1)


def mlstm(proj, gates, head_g, *, batch, heads, dk, dv, rows):
    t = proj.shape[0]
    seq = t // batch
    assert rows % ML_CHUNK == 0 and seq % rows == 0
    ns = seq // rows
    qw, vw = heads * dk, heads * dv
    assert qw * 2 == vw
    row_blk = lambda b, c: b * ns + c
    return pl.pallas_call(
        functools.partial(_mlstm_kernel, heads=heads),
        grid=(batch, ns),
        in_specs=[pl.BlockSpec((rows, qw), lambda b, c: (row_blk(b, c), 0)),
                  pl.BlockSpec((rows, qw), lambda b, c: (row_blk(b, c), 1)),
                  pl.BlockSpec((rows, vw), lambda b, c: (row_blk(b, c), 1)),
                  pl.BlockSpec((rows, vw), lambda b, c: (row_blk(b, c), 2)),
                  pl.BlockSpec((rows, LANES), lambda b, c: (row_blk(b, c), 0)),
                  pl.BlockSpec((1, vw), lambda b, c: (0, 0))],
        out_specs=pl.BlockSpec((rows, vw), lambda b, c: (row_blk(b, c), 0)),
        out_shape=jax.ShapeDtypeStruct((t, vw), BF16),
        scratch_shapes=[pltpu.VMEM((heads, dk, dv), F32),
                        pltpu.VMEM((heads, dk), F32),
                        pltpu.VMEM((1, LANES), F32),
                        pltpu.VMEM((ML_CHUNK, vw), F32)],
        compiler_params=_params(("arbitrary", "arbitrary")),
        name="mlstm",
    )(proj, proj, proj, proj, gates, head_g.reshape(1, vw))


NORM_ROWS = 512
GATE_ROWS = 1024
GMLP_ROWS = 512
MLSTM_ROWS = 128
MM_TM, MM_TN = 1024, 1024
KS_TM, KS_TN, KS_TK = 1024, 1024, 4096


def _ffn(h, hg, ss, w_up_bf16, w_down, layer, next_norm_g, extra_jobs):
    d_ff = w_up_bf16.shape[1]
    a, w_down_bf16, *extra = matmul(
        hg, w_up_bf16, n_out=d_ff, tm=MM_TM, tn=MM_TN, epilogue=_sqrelu, out_dtype=BF16,
        row_ss=ss, name="ffn_up",
        make_jobs=lambda n: [SideCast(w_down, layer, d_ff, n)] + extra_jobs(n))
    outs = matmul_ksplit_res(a, w_down_bf16, h, tm=KS_TM, tn=KS_TN, tk=KS_TK,
                             name="ffn_down", next_norm_g=next_norm_g)
    return outs, extra


def kernel(x, norm_mix, norm_ffn, gm_w_in, gm_ln_g, gm_ln_b, gm_w_s, gm_b_s, gm_w_out,
           ml_w_in, ml_b_gate, ml_head_g, ml_w_out, ffn_w_up, ffn_w_down, norm_final):
    batch, seq, d = x.shape
    h = x.reshape(batch * seq, d)
    mm = functools.partial(matmul, tm=MM_TM, tn=MM_TN)

    hn, gm_w_in_bf16 = rmsnorm(
        h, norm_mix[0], BF16, NORM_ROWS,
        make_jobs=lambda n: [SideCast(gm_w_in, 0, d, n)])
    z, gm_w_out_bf16, w_up0_bf16 = mm(
        hn, gm_w_in_bf16, n_out=gm_w_in.shape[2], epilogue=_gelu, out_dtype=BF16,
        name="gmlp_in",
        make_jobs=lambda n: [SideCast(gm_w_out, 0, d, n), SideCast(ffn_w_up, 0, d, n)])
    gated = gmlp_gate(z, gm_ln_g[0], gm_ln_b[0], gm_w_s[0], gm_b_s[0], GMLP_ROWS)
    h, hg, ss = mm(gated, gm_w_out_bf16, n_out=d, epilogue=_identity, out_dtype=F32,
                   res=h, next_norm_g=norm_ffn[0], name="gmlp_out")

    heads = ML_HEADS
    dv = ml_head_g.shape[2]
    dk = dv // 2
    n_main = 2 * heads * dk + 2 * heads * dv
    ml_w_in_t = jnp.swapaxes(ml_w_in, 1, 2)
    (h, hg, ss), (ml_w_in_t_bf16,) = _ffn(
        h, hg, ss, w_up0_bf16, ffn_w_down, 0, norm_mix[1],
        lambda n: [SideCast(ml_w_in_t, 0, n_main, n)])

    gates = mlstm_gates(hg, ss, ml_w_in_t[0, n_main:, :], ml_b_gate[0], GATE_ROWS)
    proj, ml_w_out_bf16, w_up1_bf16 = mm(
        hg, ml_w_in_t_bf16, n_out=n_main, epilogue=_identity, out_dtype=BF16,
        row_ss=ss, transposed_w=True, name="mlstm_in",
        make_jobs=lambda n: [SideCast(ml_w_out, 0, d, n), SideCast(ffn_w_up, 1, d, n)])
    hs = mlstm(proj, gates, ml_head_g[0], batch=batch, heads=heads, dk=dk, dv=dv,
               rows=MLSTM_ROWS)
    h, hg, ss = mm(hs, ml_w_out_bf16, n_out=d, epilogue=_identity, out_dtype=F32,
                   res=h, next_norm_g=norm_ffn[1], name="mlstm_out")
    (h,), _ = _ffn(h, hg, ss, w_up1_bf16, ffn_w_down, 1, None, lambda n: [])

    (out,) = rmsnorm(h, norm_final, F32, NORM_ROWS)
    return out.reshape(batch, seq, d)
```
